```python
import math
import jax
import jax.numpy as jnp
from jax import lax
import numpy as np

D_MODEL = 1024
BATCH = 2
SEQ = 8192
DEPTH = 4
DEC_BATCH = 32
DEC_SEQ = 4
PAST_LEN = 8192
PAGE_SIZE = 128

N_META = 16
ATT_HEADS = 4
ATT_HD = 64
ATT_VD = 2 * ATT_HD
ATT_W = ATT_HEADS * ATT_VD
CONV_CH = 256
CONV_W = 31
RW_HEADS = 4
RW_HD = 64
RW_W = RW_HEADS * RW_HD
RW_LORA_W = 32
RW_LORA_A = 32
RW_LORA_V = 32
RW_LORA_G = 64
RW_SHIFT = 3 * RW_W + RW_LORA_W + RW_LORA_A + RW_LORA_G
D_MIX = ATT_W + CONV_CH + RW_W
D_IN = 3 * ATT_W + 2 * CONV_CH + RW_SHIFT
Q_BLOCK = 128
PEER_HEADS = 8
PEER_NKEYS = 128
PEER_EXPERTS = PEER_NKEYS * PEER_NKEYS
PEER_QDIM = 256
PEER_TOPK = 16
PEER_BLOCK = 128
EPS = 1e-6
RW_LNX_EPS = 64e-5

kernel_name = 'hymba_diffattn_conv_rwkv7_peer_step'


def rmsnorm(x, g):
    xf = x.astype(jnp.float32)
    y = xf * lax.rsqrt(jnp.mean(xf * xf, axis=-1, keepdims=True) + EPS)
    return (y * g.astype(jnp.float32)).astype(x.dtype)


def layernorm(x, g, b, eps):
    xf = x.astype(jnp.float32)
    mu = jnp.mean(xf, axis=-1, keepdims=True)
    var = jnp.mean(jnp.square(xf - mu), axis=-1, keepdims=True)
    return (xf - mu) * lax.rsqrt(var + eps) * g.astype(jnp.float32) + b.astype(jnp.float32)


def lambda_init(l):
    return 0.8 - 0.6 * math.exp(-0.3 * l)


def diff_attn_core(q, k, v, q_pos, k_pos, lam):
    s = jnp.einsum('bqhcd,bkhcd->bhcqk', q, k).astype(jnp.float32) * (ATT_HD ** -0.5)
    mask = k_pos[None, :] <= q_pos[:, None]
    p = jax.nn.softmax(jnp.where(mask, s, -jnp.inf), axis=-1)
    p = p[:, :, 0] - lam * p[:, :, 1]
    return jnp.einsum('bhqk,bkhe->bqhe', p.astype(v.dtype), v)


def attend_prompt(q, k, v, lam):
    B, L = q.shape[:2]
    n_real = L - N_META
    nb = n_real // Q_BLOCK
    kh = k.reshape(B, L, ATT_HEADS, 2, ATT_HD)
    pos = jnp.arange(L)
    out_meta = diff_attn_core(q[:, :N_META], kh[:, :N_META], v[:, :N_META], pos[:N_META], pos[:N_META], lam)
    qb = jnp.moveaxis(q[:, N_META:].reshape(B, nb, Q_BLOCK, ATT_HEADS, 2, ATT_HD), 1, 0)

    def one_block(args):
        i, qi = args
        q_pos = N_META + i * Q_BLOCK + jnp.arange(Q_BLOCK)
        return diff_attn_core(qi, kh, v, q_pos, pos, lam)

    out = lax.map(one_block, (jnp.arange(nb), qb))
    out = jnp.moveaxis(out, 0, 1).reshape(B, n_real, ATT_HEADS, ATT_VD)
    return jnp.concatenate([out_meta, out], axis=1)


def make_attend_sample(ck, cv, page_table):
    def attend(q, k, v, lam):
        B, T = q.shape[:2]
        k_past = ck[page_table].reshape(B, -1, ATT_HEADS, ATT_VD).astype(k.dtype)
        v_past = cv[page_table].reshape(B, -1, ATT_HEADS, ATT_VD).astype(v.dtype)
        P = k_past.shape[1]
        k_all = jnp.concatenate([k_past, k], axis=1).reshape(B, P + T, ATT_HEADS, 2, ATT_HD)
        v_all = jnp.concatenate([v_past, v], axis=1)
        return diff_attn_core(q, k_all, v_all, P + jnp.arange(T), jnp.arange(P + T), lam)
    return attend


def conv_group(u, buf, W):
    a, gt = jnp.split(u, 2, axis=-1)
    glu = a * jax.nn.sigmoid(gt)
    full = jnp.concatenate([buf.astype(glu.dtype), glu], axis=1)
    y = lax.conv_general_dilated(full, W['conv_dw'][:, None, :].astype(glu.dtype), (1,), 'VALID',
                                 dimension_numbers=('NWC', 'WIO', 'NWC'),
                                 feature_group_count=CONV_CH) + W['conv_dw_b']
    y = jax.nn.silu(layernorm(y, W['conv_ln_g'], W['conv_ln_b'], 1e-5)).astype(u.dtype)
    return y, full[:, -(CONV_W - 1):]


def wkv_scan(r, w, k, v, a, b, S0):
    seq = tuple(jnp.moveaxis(t.astype(jnp.float32), 1, 0) for t in (r, w, k, v, a, b))

    def step(S, inp):
        rt, wt, kt, vt, at, bt = inp
        sa = jnp.einsum('bhvk,bhk->bhv', S, at)
        S = S * wt[:, :, None, :] + sa[..., None] * bt[:, :, None, :] + vt[..., None] * kt[:, :, None, :]
        return S, jnp.einsum('bhvk,bhk->bhv', S, rt)

    S, y = lax.scan(step, S0.astype(jnp.float32), seq)
    return jnp.moveaxis(y, 0, 1), S


def rwkv_group(cols, shift_prev, S0, v_first, W, l):
    B, T, _ = cols.shape
    prev = jnp.concatenate([shift_prev[:, None].astype(cols.dtype), cols[:, :-1]], axis=1)
    xsh = cols + (prev - cols) * W['rw_mu']
    o1, o2, o3 = RW_W, 2 * RW_W, 3 * RW_W
    o4, o5 = o3 + RW_LORA_W, o3 + RW_LORA_W + RW_LORA_A
    r, k, v = xsh[..., :o1], xsh[..., o1:o2], xsh[..., o2:o3]
    xw, xa, xg = xsh[..., o3:o4], xsh[..., o4:o5], xsh[..., o5:]
    w_log = -jax.nn.softplus(-(W['rw_w0'] + jnp.tanh(xw) @ W['rw_w2'])) - 0.5
    decay = jnp.exp(-jnp.exp(w_log.astype(jnp.float32)))
    a = jax.nn.sigmoid(W['rw_a0'] + xa @ W['rw_a2'])
    g = jax.nn.sigmoid(xg) @ W['rw_g2']
    if l == 0:
        v_first = v
    else:
        v = v + (v_first - v) * jax.nn.sigmoid(W['rw_v0'] + (v @ W['rw_v1']) @ W['rw_v2'])
    hs = lambda t: t.reshape(B, T, RW_HEADS, RW_HD)
    kk = hs((k * W['rw_kk']).astype(jnp.float32))
    kk = kk / jnp.maximum(jnp.sqrt(jnp.sum(kk * kk, axis=-1, keepdims=True)), 1e-12)
    k = k * (1.0 + (a - 1.0) * W['rw_ka'])
    ah = hs(a).astype(jnp.float32)
    y, S = wkv_scan(hs(r), hs(decay), hs(k), hs(v), -kk, kk * ah, S0)
    mu = jnp.mean(y, axis=-1, keepdims=True)
    var = jnp.mean(jnp.square(y - mu), axis=-1, keepdims=True)
    yn = ((y - mu) * lax.rsqrt(var + RW_LNX_EPS)).reshape(B, T, RW_W)
    yn = yn * W['rw_lnx_g'].astype(jnp.float32) + W['rw_lnx_b'].astype(jnp.float32)
    bonus = jnp.sum(hs(r) * hs(k) * W['rw_rk'], axis=-1, keepdims=True) * hs(v)
    out = (yn + bonus.reshape(B, T, RW_W)).astype(cols.dtype) * g
    return out, cols[:, -1], S, v_first


def peer(h, wq, k1, k2, u_tab, v_tab):
    n, d = h.shape
    pad = (-n) % PEER_BLOCK
    hb = jnp.pad(h, ((0, pad), (0, 0))).reshape(-1, PEER_BLOCK, d)
    half = PEER_QDIM // 2
    nc = PEER_TOPK * PEER_TOPK

    def one_block(hx):
        q = (hx @ wq).reshape(PEER_BLOCK, PEER_HEADS, 2, half)
        s1 = jnp.einsum('nhd,hkd->nhk', q[:, :, 0], k1).astype(jnp.float32)
        s2 = jnp.einsum('nhd,hkd->nhk', q[:, :, 1], k2).astype(jnp.float32)
        t1, i1 = lax.top_k(s1, PEER_TOPK)
        t2, i2 = lax.top_k(s2, PEER_TOPK)
        cand = (t1[..., :, None] + t2[..., None, :]).reshape(PEER_BLOCK, PEER_HEADS, nc)
        cidx = (i1[..., :, None] * PEER_NKEYS + i2[..., None, :]).reshape(PEER_BLOCK, PEER_HEADS, nc)
        ts, sel = lax.top_k(cand, PEER_TOPK)
        eidx = jnp.take_along_axis(cidx, sel, axis=-1)
        gate = jax.nn.softmax(ts, axis=-1).astype(hx.dtype)
        act = jax.nn.gelu(jnp.einsum('nd,nhed->nhe', hx, u_tab[eidx]), approximate=False)
        return jnp.einsum('nhe,nhed->nd', gate * act, v_tab[eidx])

    out = lax.map(one_block, hb).reshape(-1, d)
    return out[:n]


def run_layer(x, l, W, attend, conv_buf, shift_prev, S0, v_first):
    B, T, _ = x.shape
    h = rmsnorm(x, W['norm_mix'])
    proj = h @ W['w_in']
    o1, o2, o3, o4 = ATT_W, 2 * ATT_W, 3 * ATT_W, 3 * ATT_W + 2 * CONV_CH
    q = proj[..., :o1].reshape(B, T, ATT_HEADS, 2, ATT_HD)
    k = proj[..., o1:o2].reshape(B, T, ATT_HEADS, ATT_VD)
    v = proj[..., o2:o3].reshape(B, T, ATT_HEADS, ATT_VD)
    lam_i = lambda_init(l)
    lam = (jnp.exp(jnp.sum(W['lam_q1'] * W['lam_k1']).astype(jnp.float32))
           - jnp.exp(jnp.sum(W['lam_q2'] * W['lam_k2']).astype(jnp.float32)) + lam_i)
    att = rmsnorm(attend(q, k, v, lam), W['attn_subln']) * (1.0 - lam_i)
    conv_y, new_buf = conv_group(proj[..., o3:o4], conv_buf, W)
    rw_y, new_shift, S, v_first = rwkv_group(proj[..., o4:], shift_prev, S0, v_first, W, l)
    mix = jnp.concatenate([att.reshape(B, T, ATT_W), conv_y, rw_y], axis=-1)
    x = x + mix @ W['w_out']
    h2 = rmsnorm(x, W['norm_ffn'])
    x = x + peer(h2.reshape(B * T, D_MODEL), W['peer_wq'], W['peer_k1'], W['peer_k2'],
                 W['peer_u'], W['peer_v']).reshape(B, T, D_MODEL)
    return x, k, v, new_buf, new_shift, S, v_first


def setup_inputs(seed: int = 0) -> dict:
    key = jax.random.key(seed)
    ks = jax.random.split(key, 48)

    def nrm(i, shape, scale):
        return jax.random.normal(ks[i], shape, jnp.float32) * scale

    n_pages = PAST_LEN // PAGE_SIZE
    n_used = DEC_BATCH * n_pages
    n_pool = n_used + max(1, n_used // 4)
    page_table = jax.random.permutation(ks[0], n_pool)[:n_used].reshape(DEC_BATCH, n_pages).astype(jnp.int32)
    half = PEER_QDIM // 2
    return {
        'x_prompt': nrm(1, (BATCH, SEQ, D_MODEL), 1.0),
        'x_sample': nrm(2, (DEC_BATCH, DEC_SEQ, D_MODEL), 1.0),
        'cache_k': nrm(3, (DEPTH, n_pool, PAGE_SIZE, ATT_HEADS, 2 * ATT_HD), 1.0),
        'cache_v': nrm(4, (DEPTH, n_pool, PAGE_SIZE, ATT_HEADS, ATT_VD), 1.0),
        'state_conv': nrm(5, (DEPTH, DEC_BATCH, CONV_W - 1, CONV_CH), 0.5),
        'state_shift': nrm(6, (DEPTH, DEC_BATCH, RW_SHIFT), 0.5),
        'state_wkv': nrm(7, (DEPTH, DEC_BATCH, RW_HEADS, RW_HD, RW_HD), 0.1),
        'page_table': page_table,
        'meta_tokens': nrm(8, (N_META, D_MODEL), 1.0),
        'norm_mix': 1.0 + nrm(9, (DEPTH, D_MODEL), 0.02),
        'w_in': nrm(10, (DEPTH, D_MODEL, D_IN), D_MODEL ** -0.5),
        'w_out': nrm(11, (DEPTH, D_MIX, D_MODEL), D_MIX ** -0.5),
        'lam_q1': nrm(12, (DEPTH, ATT_HD), 0.1),
        'lam_k1': nrm(13, (DEPTH, ATT_HD), 0.1),
        'lam_q2': nrm(14, (DEPTH, ATT_HD), 0.1),
        'lam_k2': nrm(15, (DEPTH, ATT_HD), 0.1),
        'attn_subln': 1.0 + nrm(16, (DEPTH, ATT_VD), 0.02),
        'conv_dw': nrm(17, (DEPTH, CONV_W, CONV_CH), CONV_W ** -0.5),
        'conv_dw_b': nrm(18, (DEPTH, CONV_CH), 0.02),
        'conv_ln_g': 1.0 + nrm(19, (DEPTH, CONV_CH), 0.02),
        'conv_ln_b': nrm(20, (DEPTH, CONV_CH), 0.02),
        'rw_mu': jax.random.uniform(ks[21], (DEPTH, RW_SHIFT), jnp.float32),
        'rw_w0': nrm(22, (DEPTH, RW_W), 0.5) - 0.5,
        'rw_w2': nrm(23, (DEPTH, RW_LORA_W, RW_W), 0.1),
        'rw_a0': nrm(24, (DEPTH, RW_W), 0.1),
        'rw_a2': nrm(25, (DEPTH, RW_LORA_A, RW_W), 0.1),
        'rw_g2': nrm(26, (DEPTH, RW_LORA_G, RW_W), RW_LORA_G ** -0.5),
        'rw_v0': nrm(27, (DEPTH - 1, RW_W), 0.1),
        'rw_v1': nrm(28, (DEPTH - 1, RW_W, RW_LORA_V), RW_W ** -0.5),
        'rw_v2': nrm(29, (DEPTH - 1, RW_LORA_V, RW_W), 0.1),
        'rw_kk': 0.85 + nrm(30, (DEPTH, RW_W), 0.02),
        'rw_ka': 1.0 + nrm(31, (DEPTH, RW_W), 0.02),
        'rw_rk': nrm(32, (DEPTH, RW_HEADS, RW_HD), 0.1),
        'rw_lnx_g': 1.0 + nrm(33, (DEPTH, RW_W), 0.02),
        'rw_lnx_b': nrm(34, (DEPTH, RW_W), 0.02),
        'norm_ffn': 1.0 + nrm(35, (DEPTH, D_MODEL), 0.02),
        'peer_wq': nrm(36, (DEPTH, D_MODEL, PEER_HEADS * PEER_QDIM), D_MODEL ** -0.5),
        'peer_k1': nrm(37, (DEPTH, PEER_HEADS, PEER_NKEYS, half), half ** -0.5),
        'peer_k2': nrm(38, (DEPTH, PEER_HEADS, PEER_NKEYS, half), half ** -0.5),
        'peer_u': nrm(39, (DEPTH, PEER_EXPERTS, D_MODEL), D_MODEL ** -0.5),
        'peer_v': nrm(40, (DEPTH, PEER_EXPERTS, D_MODEL), 0.1),
        'norm_final': 1.0 + nrm(41, (D_MODEL,), 0.02),
    }


def reference(x_prompt, x_sample, cache_k, cache_v, state_conv, state_shift, state_wkv, page_table,
              meta_tokens, norm_mix, w_in, w_out, lam_q1, lam_k1, lam_q2, lam_k2, attn_subln,
              conv_dw, conv_dw_b, conv_ln_g, conv_ln_b,
              rw_mu, rw_w0, rw_w2, rw_a0, rw_a2, rw_g2, rw_v0, rw_v1, rw_v2, rw_kk, rw_ka, rw_rk,
              rw_lnx_g, rw_lnx_b,
              norm_ffn, peer_wq, peer_k1, peer_k2, peer_u, peer_v, norm_final):
    B = x_prompt.shape[0]
    xp = jnp.concatenate([jnp.broadcast_to(meta_tokens[None].astype(x_prompt.dtype), (B, N_META, D_MODEL)),
                          x_prompt], axis=1)
    xs = x_sample
    vf_p = None
    vf_s = None
    kp_l, vp_l, cp_l, sp_l, Sp_l = [], [], [], [], []
    ks_l, vs_l, cs_l, ss_l, Ss_l = [], [], [], [], []
    for l in range(DEPTH):
        W = {
            'norm_mix': norm_mix[l], 'w_in': w_in[l], 'w_out': w_out[l],
            'lam_q1': lam_q1[l], 'lam_k1': lam_k1[l], 'lam_q2': lam_q2[l], 'lam_k2': lam_k2[l],
            'attn_subln': attn_subln[l],
            'conv_dw': conv_dw[l], 'conv_dw_b': conv_dw_b[l], 'conv_ln_g': conv_ln_g[l], 'conv_ln_b': conv_ln_b[l],
            'rw_mu': rw_mu[l], 'rw_w0': rw_w0[l], 'rw_w2': rw_w2[l], 'rw_a0': rw_a0[l], 'rw_a2': rw_a2[l],
            'rw_g2': rw_g2[l], 'rw_kk': rw_kk[l], 'rw_ka': rw_ka[l], 'rw_rk': rw_rk[l],
            'rw_lnx_g': rw_lnx_g[l], 'rw_lnx_b': rw_lnx_b[l],
            'norm_ffn': norm_ffn[l], 'peer_wq': peer_wq[l], 'peer_k1': peer_k1[l], 'peer_k2': peer_k2[l],
            'peer_u': peer_u[l], 'peer_v': peer_v[l],
        }
        if l > 0:
            W['rw_v0'] = rw_v0[l - 1]
            W['rw_v1'] = rw_v1[l - 1]
            W['rw_v2'] = rw_v2[l - 1]
        xp, kp, vp, cp, sp, Sp, vf_p = run_layer(
            xp, l, W, attend_prompt,
            jnp.zeros((B, CONV_W - 1, CONV_CH), xp.dtype),
            jnp.zeros((B, RW_SHIFT), xp.dtype),
            jnp.zeros((B, RW_HEADS, RW_HD, RW_HD), jnp.float32), vf_p)
        xs, ks_, vs_, cs, ss, Ss, vf_s = run_layer(
            xs, l, W, make_attend_sample(cache_k[l], cache_v[l], page_table),
            state_conv[l], state_shift[l], state_wkv[l], vf_s)
        kp_l.append(kp)
        vp_l.append(vp)
        cp_l.append(cp)
        sp_l.append(sp)
        Sp_l.append(Sp.astype(x_prompt.dtype))
        ks_l.append(ks_)
        vs_l.append(vs_)
        cs_l.append(cs)
        ss_l.append(ss)
        Ss_l.append(Ss.astype(state_wkv.dtype))
    y_prompt = rmsnorm(xp, norm_final)[:, N_META:]
    y_sample = rmsnorm(xs, norm_final)
    return (y_prompt, y_sample,
            jnp.stack(kp_l), jnp.stack(vp_l), jnp.stack(cp_l), jnp.stack(sp_l), jnp.stack(Sp_l),
            jnp.stack(ks_l), jnp.stack(vs_l), jnp.stack(cs_l), jnp.stack(ss_l), jnp.stack(Ss_l))
```

```python
import functools
import math

import jax
import jax.numpy as jnp
from jax import lax
from jax.experimental import pallas as pl
from jax.experimental.pallas import tpu as pltpu

F32 = jnp.float32
BF16 = jnp.bfloat16
HI = lax.Precision.HIGHEST

EPS = 1e-6
RW_LNX_EPS = 64e-5
CONV_LN_EPS = 1e-5
PEER_TOPK = 16
LANES = 128
SUBLANES = 8
VMEM_LIMIT = 56 * 1024 * 1024
NEG_BIG = -1e30


def _lambda_init(l):
    return 0.8 - 0.6 * math.exp(-0.3 * l)


def _cparams(sem):
    return pltpu.CompilerParams(dimension_semantics=sem, vmem_limit_bytes=VMEM_LIMIT)


def _dot(a, b, precision=None):
    return jnp.dot(a, b, preferred_element_type=F32, precision=precision)


def _dot_nt(a, b, precision=None):
    return lax.dot_general(a, b, (((1,), (1,)), ((), ())), preferred_element_type=F32, precision=precision)


def _dot_tn(a, b, precision=None):
    return lax.dot_general(a, b, (((0,), (0,)), ((), ())), preferred_element_type=F32, precision=precision)


def _full(shape):
    nd = len(shape)
    return pl.BlockSpec(shape, lambda *_: (0,) * nd)


def _norm_proj_kernel(x_ref, g_ref, w_ref, qkv_ref, k_ref, v_ref, u_ref, c_ref, *, att_w, conv_w2, q_scale):
    x = x_ref[...]
    h = x * lax.rsqrt(jnp.mean(x * x, axis=-1, keepdims=True) + EPS) * g_ref[...]
    hb = h.astype(BF16)
    o1, o2, o3 = att_w, 2 * att_w, 3 * att_w
    o4 = o3 + conv_w2
    q = _dot(hb, w_ref[:, 0:o1])
    k = _dot(hb, w_ref[:, o1:o2])
    v = _dot(hb, w_ref[:, o2:o3])
    qkv_ref[:, 0:o1] = (q * q_scale).astype(BF16)
    qkv_ref[:, o1:o2] = k.astype(BF16)
    qkv_ref[:, o2:o3] = v.astype(BF16)
    k_ref[...] = k
    v_ref[...] = v
    u_ref[...] = _dot(hb, w_ref[:, o3:o4])
    c_ref[...] = _dot(hb, w_ref[:, o4:])


def _norm_proj(x, g, w_bf, *, att_w, conv_w2, q_scale, tm):
    n, d = x.shape
    d_in = w_bf.shape[1]
    shift_w = d_in - 3 * att_w - conv_w2
    tm = min(tm, n)
    row = lambda w: pl.BlockSpec((tm, w), lambda i: (i, 0))
    return pl.pallas_call(
        functools.partial(_norm_proj_kernel, att_w=att_w, conv_w2=conv_w2, q_scale=q_scale),
        grid=(pl.cdiv(n, tm),),
        in_specs=[row(d), _full((1, d)), _full((d, d_in))],
        out_specs=[row(3 * att_w), row(att_w), row(att_w), row(conv_w2), row(shift_w)],
        out_shape=[jax.ShapeDtypeStruct((n, 3 * att_w), BF16), jax.ShapeDtypeStruct((n, att_w), F32),
                   jax.ShapeDtypeStruct((n, att_w), F32), jax.ShapeDtypeStruct((n, conv_w2), F32),
                   jax.ShapeDtypeStruct((n, shift_w), F32)],
        compiler_params=_cparams(("parallel",)),
        name="norm_proj",
    )(x, g.reshape(1, d), w_bf)


def _softmax_step(carry, q1, q2, kt, vt, hd, mask):
    m1, l1, a1, m2, l2, a2 = carry
    out = []
    for (m, l, a, q, ks) in ((m1, l1, a1, q1, kt[:, :hd]), (m2, l2, a2, q2, kt[:, hd:])):
        s = _dot_nt(q, ks)
        if mask is not None:
            s = jnp.where(mask, s, NEG_BIG)
        mn = jnp.maximum(m, jnp.max(s, axis=-1, keepdims=True))
        alpha = jnp.exp(m - mn)
        p = jnp.exp(s - mn)
        l = alpha * l + jnp.sum(p, axis=-1, keepdims=True)
        a = alpha * a + _dot(p.astype(BF16), vt)
        out += [mn, l, a]
    return tuple(out)


def _attn_prompt_kernel(lam_ref, q_ref, k_ref, v_ref, g_ref, o_ref, *, n_meta, tq, hd, lam_init):
    i = pl.program_id(2)
    lam = lam_ref[0]
    vd = v_ref.shape[1]

    def finish(carry):
        m1, l1, a1, m2, l2, a2 = carry
        o = a1 / l1 - lam * (a2 / l2)
        o = o * lax.rsqrt(jnp.mean(o * o, axis=-1, keepdims=True) + EPS) * g_ref[...]
        return o * (1.0 - lam_init)

    def init(rows):
        z = jnp.zeros((rows, 1), F32)
        za = jnp.zeros((rows, vd), F32)
        return (z + NEG_BIG, z, za, z + NEG_BIG, z, za)

    def tri(rows):
        r = lax.broadcasted_iota(jnp.int32, (rows, rows), 0)
        c = lax.broadcasted_iota(jnp.int32, (rows, rows), 1)
        return c <= r

    @pl.when(i == 0)
    def _():
        q = q_ref[0:n_meta, :]
        carry = _softmax_step(init(n_meta), q[:, :hd], q[:, hd:], k_ref[0:n_meta, :], v_ref[0:n_meta, :], hd,
                              tri(n_meta))
        o_ref[0:n_meta, :] = finish(carry)

    @pl.when(i > 0)
    def _():
        r0 = pl.multiple_of(n_meta + (i - 1) * tq, 16)
        q = q_ref[pl.ds(r0, tq), :]
        q1, q2 = q[:, :hd], q[:, hd:]
        carry = _softmax_step(init(tq), q1, q2, k_ref[0:n_meta, :], v_ref[0:n_meta, :], hd, None)

        def body(j, c):
            s0 = pl.multiple_of(n_meta + j * tq, 16)
            return _softmax_step(c, q1, q2, k_ref[pl.ds(s0, tq), :], v_ref[pl.ds(s0, tq), :], hd, None)

        carry = lax.fori_loop(0, i - 1, body, carry)
        carry = _softmax_step(carry, q1, q2, k_ref[pl.ds(r0, tq), :], v_ref[pl.ds(r0, tq), :], hd, tri(tq))
        o_ref[pl.ds(r0, tq), :] = finish(carry)


def _attn_prompt(qkv_bf, lam, subln, *, batch, heads, n_meta, tq, lam_init):
    n, w3 = qkv_bf.shape
    lp = n // batch
    vd = w3 // 3 // heads
    nq = 1 + (lp - n_meta) // tq
    seq = lambda off: pl.BlockSpec((lp, vd), lambda b, h, i: (b, off + h))
    return pl.pallas_call(
        functools.partial(_attn_prompt_kernel, n_meta=n_meta, tq=tq, hd=vd // 2, lam_init=lam_init),
        grid=(batch, heads, nq),
        in_specs=[pl.BlockSpec(memory_space=pltpu.SMEM), seq(0), seq(heads), seq(2 * heads),
                  pl.BlockSpec((1, vd), lambda b, h, i: (0, 0))],
        out_specs=seq(0),
        out_shape=jax.ShapeDtypeStruct((n, heads * vd), F32),
        compiler_params=_cparams(("parallel", "parallel", "arbitrary")),
        name="attn_prompt",
    )(lam, qkv_bf, qkv_bf, qkv_bf, subln.reshape(1, vd))


def _attn_sample_kernel(pt_ref, lam_ref, q_ref, kn_ref, vn_ref, g_ref, *refs, pages, t_new, heads, lam_init):
    k_refs = refs[:pages]
    v_refs = refs[pages:2 * pages]
    o_ref = refs[2 * pages]
    m_sc, l_sc, acc_sc = refs[2 * pages + 1:]
    s_idx = pl.program_id(1)
    ncol = q_ref.shape[2]
    vd = o_ref.shape[3]

    @pl.when(s_idx == 0)
    def _():
        m_sc[...] = jnp.full(m_sc.shape, NEG_BIG, F32)
        l_sc[...] = jnp.zeros(l_sc.shape, F32)
        acc_sc[...] = jnp.zeros(acc_sc.shape, F32)

    eye = (lax.broadcasted_iota(jnp.int32, (ncol, ncol), 0) == lax.broadcasted_iota(jnp.int32, (ncol, ncol), 1))

    def to_col(row):
        return jnp.sum(jnp.where(eye, jnp.broadcast_to(row, (ncol, ncol)), 0.0), axis=1, keepdims=True)

    def update(st, vt, mask):
        if mask is not None:
            st = jnp.where(mask, st, NEG_BIG)
        m_old = m_sc[...]
        m_new = jnp.maximum(m_old, jnp.max(st, axis=0, keepdims=True))
        alpha = jnp.exp(m_old - m_new)
        p = jnp.exp(st - m_new)
        l_sc[...] = alpha * l_sc[...] + jnp.sum(p, axis=0, keepdims=True)
        acc_sc[...] = to_col(alpha) * acc_sc[...] + _dot_tn(p.astype(BF16), vt)
        m_sc[...] = m_new

    qpp = q_ref[0]
    kcat = jnp.concatenate([r[...] for r in k_refs], axis=0).astype(BF16)
    vcat = jnp.concatenate([r[...] for r in v_refs], axis=0).astype(BF16)
    update(_dot(kcat, qpp), vcat, None)

    @pl.when(s_idx == pl.num_programs(1) - 1)
    def _():
        lam = lam_ref[0]
        kn = kn_ref[0].astype(BF16)
        rows = kn.shape[0]
        st = _dot(kn, qpp)
        r = lax.broadcasted_iota(jnp.int32, (rows, ncol), 0)
        c = lax.broadcasted_iota(jnp.int32, (rows, ncol), 1)
        mask = (r < t_new) & (r <= (c & (t_new - 1)))
        update(st, vn_ref[0].astype(BF16), mask)
        out = acc_sc[...] / to_col(l_sc[...])
        for h in range(heads):
            blk = out[2 * t_new * h:2 * t_new * (h + 1), vd * h:vd * (h + 1)]
            o = blk[0:t_new] - lam * blk[t_new:2 * t_new]
            o = o * lax.rsqrt(jnp.mean(o * o, axis=-1, keepdims=True) + EPS) * g_ref[...]
            o_ref[0, h] = o * (1.0 - lam_init)


def _attn_sample(page_table, lam, qpp, k_new, v_new, subln, cache_k, cache_v, *, layer, pages, t_new, heads,
                 lam_init):
    bs, n_pages = page_table.shape
    page, kw = cache_k.shape[2], cache_k.shape[3]
    vd = kw // heads
    ncol = qpp.shape[2]
    steps = n_pages // pages

    def page_spec(p):
        return pl.BlockSpec((None, None, page, kw), lambda b, s, pt: (layer, pt[b, s * pages + p], 0, 0))

    per_seq = lambda shape: pl.BlockSpec((1,) + shape, lambda b, s, pt: (b,) + (0,) * len(shape))
    grid_spec = pltpu.PrefetchScalarGridSpec(
        num_scalar_prefetch=1,
        grid=(bs, steps),
        in_specs=[pl.BlockSpec(memory_space=pltpu.SMEM), per_seq(qpp.shape[1:]), per_seq(k_new.shape[1:]),
                  per_seq(v_new.shape[1:]), pl.BlockSpec((1, vd), lambda b, s, pt: (0, 0))]
                 + [page_spec(p) for p in range(pages)] * 2,
        out_specs=per_seq((heads, t_new, vd)),
        scratch_shapes=[pltpu.VMEM((1, ncol), F32), pltpu.VMEM((1, ncol), F32), pltpu.VMEM((ncol, kw), F32)],
    )
    return pl.pallas_call(
        functools.partial(_attn_sample_kernel, pages=pages, t_new=t_new, heads=heads, lam_init=lam_init),
        grid_spec=grid_spec,
        out_shape=jax.ShapeDtypeStruct((bs, heads, t_new, vd), F32),
        compiler_params=_cparams(("parallel", "arbitrary")),
        name="attn_sample",
    )(page_table, lam, qpp, k_new, v_new, subln.reshape(1, vd), *([cache_k] * pages), *([cache_v] * pages))


def _conv_post(acc, g, b):
    mu = jnp.mean(acc, axis=-1, keepdims=True)
    d = acc - mu
    var = jnp.mean(d * d, axis=-1, keepdims=True)
    y = d * lax.rsqrt(var + CONV_LN_EPS) * g + b
    return y * jax.nn.sigmoid(y)


def _conv_prompt_kernel(u_ref, dw_ref, b_ref, g_ref, bb_ref, y_ref, nb_ref, ext, *, tt, rc, taps, halo):
    i = pl.program_id(1)
    ch = y_ref.shape[1]
    lead = halo - (taps - 1)

    @pl.when(i == 0)
    def _():
        ext[0:halo, :] = jnp.zeros((halo, ch), F32)

    @pl.when(i > 0)
    def _():
        ext[0:halo, :] = ext[tt:tt + halo, :]

    u = u_ref[...]
    ext[halo:halo + tt, :] = u[:, :ch] * jax.nn.sigmoid(u[:, ch:])
    for c0 in range(0, tt, rc):
        acc = jnp.broadcast_to(b_ref[...], (rc, ch))
        for j in range(taps):
            acc = acc + ext[c0 + lead + j:c0 + lead + j + rc, :] * dw_ref[j:j + 1, :]
        y_ref[c0:c0 + rc, :] = _conv_post(acc, g_ref[...], bb_ref[...])

    @pl.when(i == pl.num_programs(1) - 1)
    def _():
        nb_ref[0] = ext[tt + lead:tt + halo, :]


def _conv_prompt(u, dw, b, g, bb, *, batch, tt, rc):
    n, w2 = u.shape
    ch = w2 // 2
    taps = dw.shape[0]
    halo = 32
    lp = n // batch
    nt = lp // tt
    vec = lambda: pl.BlockSpec((1, ch), lambda bi, i: (0, 0))
    return pl.pallas_call(
        functools.partial(_conv_prompt_kernel, tt=tt, rc=rc, taps=taps, halo=halo),
        grid=(batch, nt),
        in_specs=[pl.BlockSpec((tt, w2), lambda bi, i: (bi * nt + i, 0)),
                  pl.BlockSpec((taps, ch), lambda bi, i: (0, 0)), vec(), vec(), vec()],
        out_specs=[pl.BlockSpec((tt, ch), lambda bi, i: (bi * nt + i, 0)),
                   pl.BlockSpec((1, taps - 1, ch), lambda bi, i: (bi, 0, 0))],
        out_shape=[jax.ShapeDtypeStruct((n, ch), F32), jax.ShapeDtypeStruct((batch, taps - 1, ch), F32)],
        scratch_shapes=[pltpu.VMEM((halo + tt, ch), F32)],
        compiler_params=_cparams(("parallel", "arbitrary")),
        name="conv_prompt",
    )(u, dw, b.reshape(1, ch), g.reshape(1, ch), bb.reshape(1, ch))


def _conv_sample_kernel(u_ref, buf_ref, dw_ref, b_ref, g_ref, bb_ref, y_ref, nb_ref):
    t_new, bs, ch = y_ref.shape
    taps = dw_ref.shape[0]
    full = [buf_ref[j] for j in range(taps - 1)]
    for t in range(t_new):
        u = u_ref[t]
        full.append(u[:, :ch] * jax.nn.sigmoid(u[:, ch:]))
    for t in range(t_new):
        acc = jnp.broadcast_to(b_ref[...], (bs, ch))
        for j in range(taps):
            acc = acc + full[t + j] * dw_ref[j:j + 1, :]
        y_ref[t] = _conv_post(acc, g_ref[...], bb_ref[...])
    for j in range(taps - 1):
        nb_ref[j] = full[t_new + j]


def _conv_sample(u_tm, buf_tm, dw, b, g, bb):
    t_new, bs, w2 = u_tm.shape
    ch = w2 // 2
    taps = dw.shape[0]
    return pl.pallas_call(
        _conv_sample_kernel,
        in_specs=[_full(u_tm.shape), _full(buf_tm.shape), _full(dw.shape), _full((1, ch)), _full((1, ch)),
                  _full((1, ch))],
        out_specs=[_full((t_new, bs, ch)), _full((taps - 1, bs, ch))],
        out_shape=[jax.ShapeDtypeStruct((t_new, bs, ch), F32), jax.ShapeDtypeStruct((taps - 1, bs, ch), F32)],
        grid=(1,),
        name="conv_sample",
    )(u_tm, buf_tm, dw, b.reshape(1, ch), g.reshape(1, ch), bb.reshape(1, ch))


def _head_ones(w, hd):
    r = lax.broadcasted_iota(jnp.int32, (w, w), 0)
    c = lax.broadcasted_iota(jnp.int32, (w, w), 1)
    shift = hd.bit_length() - 1
    return (lax.shift_right_logical(r, shift) == lax.shift_right_logical(c, shift)).astype(F32)


def _rw_prep_kernel(*refs, rw_w, hd, mix_v):
    if mix_v:
        (c_ref, p_ref, vf_ref, mu_ref, w0_ref, w2_ref, a0_ref, a2_ref, g2_ref, kk_ref, ka_ref,
         v0_ref, v1_ref, v2_ref, o_ref) = refs
    else:
        (c_ref, p_ref, mu_ref, w0_ref, w2_ref, a0_ref, a2_ref, g2_ref, kk_ref, ka_ref, o_ref) = refs
    cols = c_ref[...]
    xsh = cols + (p_ref[...] - cols) * mu_ref[...]
    r = xsh[:, 0:rw_w]
    k = xsh[:, rw_w:2 * rw_w]
    v = xsh[:, 2 * rw_w:3 * rw_w]
    lora = xsh[:, 3 * rw_w:]
    w_log = -jax.nn.softplus(-(w0_ref[...] + _dot(jnp.tanh(lora), w2_ref[...], HI))) - 0.5
    a = jax.nn.sigmoid(a0_ref[...] + _dot(lora, a2_ref[...], HI))
    g = _dot(jax.nn.sigmoid(lora), g2_ref[...], HI)
    if mix_v:
        mixw = jax.nn.sigmoid(v0_ref[...] + _dot(_dot(v, v1_ref[...], HI), v2_ref[...], HI))
        v = v + (vf_ref[...] - v) * mixw
    kk = k * kk_ref[...]
    n2 = _dot(kk * kk, _head_ones(rw_w, hd), HI)
    kk = kk / jnp.maximum(jnp.sqrt(n2), 1e-12)
    k2 = k * (1.0 + (a - 1.0) * ka_ref[...])
    for idx, val in enumerate((r, -jnp.exp(w_log), k2, v, kk, a, g)):
        o_ref[:, idx * rw_w:(idx + 1) * rw_w] = val


def _rw_prep(cols, prev, vfirst_src, p, *, rw_w, hd, tm):
    n, sw = cols.shape
    tm = min(tm, n)
    lw = sw - 3 * rw_w
    mix_v = vfirst_src is not None
    row = lambda w: pl.BlockSpec((tm, w), lambda i: (i, 0))
    vec = lambda a: a.reshape(1, -1)
    args = [cols, prev]
    specs = [row(sw), row(sw)]
    if mix_v:
        args.append(vfirst_src)
        specs.append(pl.BlockSpec((tm, rw_w), lambda i: (i, 3)))
    consts = [vec(p["rw_mu"]), vec(p["rw_w0"]), p["w2p"], vec(p["rw_a0"]), p["a2p"], p["g2p"], vec(p["rw_kk"]),
              vec(p["rw_ka"])]
    if mix_v:
        consts += [vec(p["rw_v0"]), p["v1p"], p["v2p"]]
    args += consts
    specs += [_full(c.shape) for c in consts]
    del lw
    return pl.pallas_call(
        functools.partial(_rw_prep_kernel, rw_w=rw_w, hd=hd, mix_v=mix_v),
        grid=(pl.cdiv(n, tm),),
        in_specs=specs,
        out_specs=row(7 * rw_w),
        out_shape=jax.ShapeDtypeStruct((n, 7 * rw_w), F32),
        compiler_params=_cparams(("parallel",)),
        name="rw_prep",
    )(*args)


def _wkv_kernel(pk_ref, s0_ref, rk_ref, lg_ref, lb_ref, y_ref, so_ref, s_sc, *, chunk, heads, hd, seq_len):
    j = pl.program_id(1)
    sb = pk_ref.shape[0]
    rw_w = heads * hd
    c2 = 2 * chunk

    @pl.when(j == 0)
    def _():
        s_sc[...] = s0_ref[...]

    row = lax.broadcasted_iota(jnp.int32, (chunk, 1), 0)
    valid = (row + j * chunk) < seq_len
    r_i = lax.broadcasted_iota(jnp.int32, (chunk, chunk), 0)
    c_i = lax.broadcasted_iota(jnp.int32, (chunk, chunk), 1)
    low_incl = (c_i <= r_i)
    tril_ones = low_incl.astype(F32)
    eye = (c_i == r_i).astype(F32)
    rr = lax.broadcasted_iota(jnp.int32, (c2, c2), 0)
    cc = lax.broadcasted_iota(jnp.int32, (c2, c2), 1)
    keep = (cc & (chunk - 1)) <= jnp.where(rr < chunk, rr - 1, rr - chunk)
    n_sq = chunk.bit_length() - 2

    for s in range(sb):
        blk = jnp.where(valid, pk_ref[s], 0.0)
        outs = []
        for h in range(heads):
            sl = lambda idx: blk[:, idx * rw_w + h * hd: idx * rw_w + (h + 1) * hd]
            r, nlw, k, v, kk, a, g = (sl(t) for t in range(7))
            cum = _dot(tril_ones, nlw, HI)
            p_inc = jnp.exp(cum)
            p_exc = jnp.exp(cum - nlw)
            p_inv = jnp.exp(-cum)
            ar = jnp.concatenate([-kk * p_exc, r * p_inc], axis=0)
            bk = jnp.concatenate([kk * a * p_inv, k * p_inv], axis=0)
            m = jnp.where(keep, _dot_nt(ar, bk, HI), 0.0)
            s_prev = s_sc[s, h]
            ars = _dot_nt(ar, s_prev, HI)
            x = m[:chunk, :chunk]
            inv = eye + x
            for _ in range(n_sq):
                x = _dot(x, x, HI)
                inv = inv + _dot(inv, x, HI)
            u = _dot(inv, ars[:chunk] + _dot(m[:chunk, chunk:], v, HI), HI)
            uv = jnp.concatenate([u, v], axis=0)
            y = ars[chunk:] + _dot(m[chunk:], uv, HI)
            s_sc[s, h] = (s_prev + _dot_tn(uv, bk, HI)) * p_inc[chunk - 1:chunk, :]
            mu = jnp.mean(y, axis=-1, keepdims=True)
            d = y - mu
            var = jnp.mean(d * d, axis=-1, keepdims=True)
            yn = d * lax.rsqrt(var + RW_LNX_EPS) * lg_ref[:, h * hd:(h + 1) * hd] + lb_ref[:, h * hd:(h + 1) * hd]
            bonus = jnp.sum(r * k * rk_ref[:, h * hd:(h + 1) * hd], axis=-1, keepdims=True) * v
            outs.append((yn + bonus) * g)
        y_ref[s] = jnp.concatenate(outs, axis=-1)

    @pl.when(j == pl.num_programs(1) - 1)
    def _():
        so_ref[...] = s_sc[...]


def _wkv(pk, s0, rk, lg, lb, *, chunk, sb):
    n_seq, seq_len, w7 = pk.shape
    heads, hd = s0.shape[1], s0.shape[2]
    rw_w = heads * hd
    vec = lambda: pl.BlockSpec((1, rw_w), lambda i, j: (0, 0))
    return pl.pallas_call(
        functools.partial(_wkv_kernel, chunk=chunk, heads=heads, hd=hd, seq_len=seq_len),
        grid=(n_seq // sb, pl.cdiv(seq_len, chunk)),
        in_specs=[pl.BlockSpec((sb, chunk, w7), lambda i, j: (i, j, 0)),
                  pl.BlockSpec((sb, heads, hd, hd), lambda i, j: (i, 0, 0, 0)), vec(), vec(), vec()],
        out_specs=[pl.BlockSpec((sb, chunk, rw_w), lambda i, j: (i, j, 0)),
                   pl.BlockSpec((sb, heads, hd, hd), lambda i, j: (i, 0, 0, 0))],
        out_shape=[jax.ShapeDtypeStruct((n_seq, seq_len, rw_w), F32),
                   jax.ShapeDtypeStruct((n_seq, heads, hd, hd), F32)],
        scratch_shapes=[pltpu.VMEM((sb, heads, hd, hd), F32)],
        compiler_params=_cparams(("parallel", "arbitrary")),
        name="wkv",
    )(pk, s0, rk.reshape(1, rw_w), lg.reshape(1, rw_w), lb.reshape(1, rw_w))


def _out_proj_kernel(x_ref, a_ref, c_ref, r_ref, wo_ref, g_ref, wq_ref, x2_ref, h2_ref, q_ref):
    aw, cw = a_ref.shape[1], c_ref.shape[1]
    x2 = (x_ref[...] + _dot(a_ref[...].astype(BF16), wo_ref[0:aw, :])
          + _dot(c_ref[...].astype(BF16), wo_ref[aw:aw + cw, :])
          + _dot(r_ref[...].astype(BF16), wo_ref[aw + cw:, :]))
    x2_ref[...] = x2
    h2 = (x2 * lax.rsqrt(jnp.mean(x2 * x2, axis=-1, keepdims=True) + EPS) * g_ref[...]).astype(BF16)
    h2_ref[...] = h2
    q_ref[...] = _dot(h2, wq_ref[...]).astype(BF16)


def _out_proj(x, att, cy, ry, wo_bf, g, wq_bf, *, tm):
    n, d = x.shape
    tm = min(tm, n)
    qw = wq_bf.shape[1]
    row = lambda w: pl.BlockSpec((tm, w), lambda i: (i, 0))
    return pl.pallas_call(
        _out_proj_kernel,
        grid=(pl.cdiv(n, tm),),
        in_specs=[row(d), row(att.shape[1]), row(cy.shape[1]), row(ry.shape[1]), _full(wo_bf.shape), _full((1, d)),
                  _full(wq_bf.shape)],
        out_specs=[row(d), row(d), row(qw)],
        out_shape=[jax.ShapeDtypeStruct((n, d), F32), jax.ShapeDtypeStruct((n, d), BF16),
                   jax.ShapeDtypeStruct((n, qw), BF16)],
        compiler_params=_cparams(("parallel",)),
        name="out_proj",
    )(x, att, cy, ry, wo_bf, g.reshape(1, d), wq_bf)


def _vmax(a, b):
    if a is None:
        return b
    if b is None:
        return a
    return jnp.maximum(a, b)


def _cmpx(xs, i, j):
    a, b = xs[i], xs[j]
    if a is None or b is None:
        xs[i], xs[j] = _vmax(a, b), None
    else:
        xs[i], xs[j] = jnp.maximum(a, b), jnp.minimum(a, b)


def _bitonic_merge_desc(xs):
    n = len(xs)
    stride = n // 2
    while stride >= 1:
        for i in range(n):
            if (i & stride) == 0:
                _cmpx(xs, i, i + stride)
        stride //= 2


def _sort_desc(xs):
    n = len(xs)
    size = 2
    while size <= n:
        for base in range(0, n, size):
            half = size // 2
            for t in range(half):
                _cmpx(xs, base + t, base + size - 1 - t)
            for sub in (base, base + half):
                seg = xs[sub:sub + half]
                _bitonic_merge_desc(seg)
                xs[sub:sub + half] = seg
        size *= 2


def _merge_top(a, b):
    n = len(a)
    z = [_vmax(a[t], b[n - 1 - t]) for t in range(n)]
    _bitonic_merge_desc(z)
    return z


def _top_sorted(x_rows):
    xs = list(x_rows)
    _sort_desc(xs)
    shift = SUBLANES // 2
    while shift >= 1:
        xs = _merge_top(xs, [pltpu.roll(t, shift, 0) for t in xs])
        shift //= 2
    return xs


def _peer_kernel(x2_ref, h2_ref, q_ref, k1_ref, k2_ref, u_ref, v_ref, o_ref,
                 s1_sc, s2_sc, thr_sc, a_sc, w_sc, acc_sc, *, heads, topk, nkeys):
    c = pl.program_id(1)
    tn = h2_ref.shape[0]
    groups = tn // LANES
    rows_per_chunk = u_ref.shape[0] // nkeys
    kslabs = nkeys // SUBLANES
    half = k1_ref.shape[2]

    @pl.when(c == 0)
    def _():
        acc_sc[...] = jnp.zeros(acc_sc.shape, F32)
        sub = lax.broadcasted_iota(jnp.int32, (SUBLANES, LANES), 0)

        def group(gi, carry):
            r0 = pl.multiple_of(gi * LANES, LANES)
            qg = q_ref[pl.ds(r0, LANES), :]
            t1 = [None] * topk
            t2 = [None] * topk
            for h in range(heads):
                for side, (k_ref, s_sc, tl) in enumerate(((k1_ref, s1_sc, t1), (k2_ref, s2_sc, t2))):
                    c0 = (2 * h + side) * half
                    st = _dot_nt(k_ref[h], qg[:, c0:c0 + half])
                    s_sc[h, :, pl.ds(r0, LANES)] = st
                    top = _top_sorted([st[SUBLANES * r:SUBLANES * (r + 1), :] for r in range(kslabs)])[:topk]
                    for t in range(topk):
                        tl[t] = top[t] if h == 0 else jnp.where(sub == h, top[t], tl[t])
            best = None
            for a_i in range(topk):
                nb = topk // (a_i + 1)
                lst = [t1[a_i] + t2[b_i] for b_i in range(nb)] + [None] * (topk - nb)
                best = lst if best is None else _merge_top(best, lst)
            z = jnp.zeros((SUBLANES, LANES), F32)
            for t in range(topk):
                z = z + jnp.exp(best[t] - best[0])
            thr_sc[0, :, pl.ds(r0, LANES)] = best[topk - 1]
            thr_sc[1, :, pl.ds(r0, LANES)] = best[0]
            thr_sc[2, :, pl.ds(r0, LANES)] = 1.0 / z
            return carry

        lax.fori_loop(0, groups, group, 0)

    a_sc[...] = _dot_nt(u_ref[...], h2_ref[...])

    def dense_group(gi, carry):
        r0 = pl.multiple_of(gi * LANES, LANES)
        lanes = pl.ds(r0, LANES)
        i1_base = pl.multiple_of(c * rows_per_chunk, SUBLANES)
        for r in range(rows_per_chunk):
            a_blk = a_sc[r * nkeys:(r + 1) * nkeys, lanes]
            act = 0.5 * a_blk * (1.0 + lax.erf(a_blk * math.sqrt(0.5)))
            gsum = jnp.zeros((nkeys, LANES), F32)
            for h in range(heads):
                s1_rows = s1_sc[h, pl.ds(i1_base + (r // SUBLANES) * SUBLANES, SUBLANES), lanes]
                cand = s1_rows[r % SUBLANES:r % SUBLANES + 1, :] + s2_sc[h, :, lanes]
                gate = jnp.exp(cand - thr_sc[1, h:h + 1, lanes]) * thr_sc[2, h:h + 1, lanes]
                gsum = gsum + jnp.where(cand >= thr_sc[0, h:h + 1, lanes], gate, 0.0)
            w_sc[r * nkeys:(r + 1) * nkeys, lanes] = (gsum * act).astype(BF16)
        return carry

    lax.fori_loop(0, groups, dense_group, 0)
    acc_sc[...] += _dot_tn(w_sc[...], v_ref[...])

    @pl.when(c == pl.num_programs(1) - 1)
    def _():
        o_ref[...] = x2_ref[...] + acc_sc[...]


def _peer(x2, h2, q, k1_bf, k2_bf, u_bf, v_bf, *, tn, ec):
    n, d = x2.shape
    heads, nkeys, half = k1_bf.shape
    n_exp = u_bf.shape[0]
    tn = min(tn, max(LANES, pl.cdiv(n, LANES) * LANES))
    assert ec % (SUBLANES * nkeys) == 0 and n_exp % ec == 0
    tok = lambda w: pl.BlockSpec((tn, w), lambda i, c: (i, 0))
    return pl.pallas_call(
        functools.partial(_peer_kernel, heads=heads, topk=PEER_TOPK, nkeys=nkeys),
        grid=(pl.cdiv(n, tn), n_exp // ec),
        in_specs=[tok(d), tok(d), tok(q.shape[1]),
                  pl.BlockSpec(k1_bf.shape, lambda i, c: (0, 0, 0)), pl.BlockSpec(k2_bf.shape, lambda i, c: (0, 0, 0)),
                  pl.BlockSpec((ec, d), lambda i, c: (c, 0)), pl.BlockSpec((ec, d), lambda i, c: (c, 0))],
        out_specs=tok(d),
        out_shape=jax.ShapeDtypeStruct((n, d), F32),
        scratch_shapes=[pltpu.VMEM((heads, nkeys, tn), F32), pltpu.VMEM((heads, nkeys, tn), F32),
                        pltpu.VMEM((3, SUBLANES, tn), F32), pltpu.VMEM((ec, tn), F32), pltpu.VMEM((ec, tn), BF16),
                        pltpu.VMEM((tn, d), F32)],
        compiler_params=_cparams(("parallel", "arbitrary")),
        name="peer",
    )(x2, h2, q, k1_bf, k2_bf, u_bf, v_bf)


def _rmsnorm_kernel(x_ref, g_ref, o_ref):
    x = x_ref[...]
    o_ref[...] = x * lax.rsqrt(jnp.mean(x * x, axis=-1, keepdims=True) + EPS) * g_ref[...]


def _rmsnorm(x, g, *, tm):
    n, d = x.shape
    tm = min(tm, n)
    return pl.pallas_call(
        _rmsnorm_kernel,
        grid=(pl.cdiv(n, tm),),
        in_specs=[pl.BlockSpec((tm, d), lambda i: (i, 0)), _full((1, d))],
        out_specs=pl.BlockSpec((tm, d), lambda i: (i, 0)),
        out_shape=jax.ShapeDtypeStruct((n, d), F32),
        compiler_params=_cparams(("parallel",)),
        name="final_norm",
    )(x, g.reshape(1, d))


def _pad_rows(w, total, offset):
    return jnp.zeros((total, w.shape[1]), w.dtype).at[offset:offset + w.shape[0]].set(w)


def kernel(x_prompt, x_sample, cache_k, cache_v, state_conv, state_shift, state_wkv, page_table, meta_tokens, norm_mix, w_in, w_out, lam_q1, lam_k1, lam_q2, lam_k2, attn_subln, conv_dw, conv_dw_b, conv_ln_g, conv_ln_b, rw_mu, rw_w0, rw_w2, rw_a0, rw_a2, rw_g2, rw_v0, rw_v1, rw_v2, rw_kk, rw_ka, rw_rk, rw_lnx_g, rw_lnx_b, norm_ffn, peer_wq, peer_k1, peer_k2, peer_u, peer_v, norm_final):
    batch, seq, d = x_prompt.shape
    bs, t_new, _ = x_sample.shape
    depth = w_in.shape[0]
    n_meta = meta_tokens.shape[0]
    lp = n_meta + seq
    att_heads, att_vd = cache_v.shape[3], cache_v.shape[4]
    att_w = att_heads * att_vd
    hd_att = att_vd // 2
    conv_ch = conv_dw.shape[2]
    rw_heads, rw_hd = state_wkv.shape[2], state_wkv.shape[3]
    rw_w = rw_heads * rw_hd
    lw, la = rw_w2.shape[1], rw_a2.shape[1]
    lg = rw_g2.shape[1]
    lora_w = lw + la + lg
    n_pool, page = cache_k.shape[1], cache_k.shape[2]
    n_exp = peer_u.shape[1]

    tm = 512
    tq = min(256, seq)
    conv_tt = next(t for t in (912, 456, 152, 72, 24, 8) if lp % t == 0)
    conv_rc = next(r for r in (48, 24, 8) if conv_tt % r == 0)
    chunk_p = min(64, 1 << (lp.bit_length() - 1))
    chunk_s = max(SUBLANES, 1 << (t_new - 1).bit_length())
    pages = min(8, page_table.shape[1])
    peer_ec = min(1024, n_exp)

    xp = jnp.concatenate([jnp.broadcast_to(meta_tokens[None], (batch, n_meta, d)), x_prompt], axis=1)
    xp = xp.reshape(batch * lp, d)
    xs = x_sample.transpose(1, 0, 2).reshape(t_new * bs, d)

    ck = cache_k.reshape(depth, n_pool, page, att_w)
    cv = cache_v.reshape(depth, n_pool, page, att_w)

    outs = {name: [] for name in ("kp", "vp", "cp", "sp", "Sp", "ks", "vs", "cs", "ss", "Ss")}
    pk0_p = pk0_s = None
    eye_h = jnp.eye(att_heads, dtype=BF16)
    eye_m = jnp.eye(2, dtype=BF16)
    ncol = max(LANES, att_heads * 2 * t_new)

    for l in range(depth):
        lam_i = _lambda_init(l)
        lam = (jnp.exp(jnp.sum(lam_q1[l] * lam_k1[l])) - jnp.exp(jnp.sum(lam_q2[l] * lam_k2[l])) + lam_i)
        lam = lam.astype(F32).reshape(1)
        w_in_bf = w_in[l].astype(BF16)
        wo_bf = w_out[l].astype(BF16)
        wq_bf = peer_wq[l].astype(BF16)
        k1_bf = peer_k1[l].astype(BF16)
        k2_bf = peer_k2[l].astype(BF16)
        u_bf = peer_u[l].astype(BF16)
        v_bf = peer_v[l].astype(BF16)
        rwp = {"rw_mu": rw_mu[l], "rw_w0": rw_w0[l], "rw_a0": rw_a0[l], "rw_kk": rw_kk[l], "rw_ka": rw_ka[l],
               "w2p": _pad_rows(rw_w2[l], lora_w, 0), "a2p": _pad_rows(rw_a2[l], lora_w, lw),
               "g2p": _pad_rows(rw_g2[l], lora_w, lw + la)}
        if l > 0:
            lv = rw_v1.shape[2]
            rwp["rw_v0"] = rw_v0[l - 1]
            rwp["v1p"] = jnp.zeros((rw_w, LANES), F32).at[:, :lv].set(rw_v1[l - 1])
            rwp["v2p"] = _pad_rows(rw_v2[l - 1], LANES, 0)

        proj = functools.partial(_norm_proj, g=norm_mix[l], w_bf=w_in_bf, att_w=att_w, conv_w2=2 * conv_ch,
                                 q_scale=hd_att ** -0.5, tm=tm)
        qkv_p, k_p, v_p, u_p, c_p = proj(xp)
        qkv_s, k_s, v_s, u_s, c_s = proj(xs)

        att_p = _attn_prompt(qkv_p, lam, attn_subln[l], batch=batch, heads=att_heads, n_meta=n_meta, tq=tq,
                             lam_init=lam_i)
        q_s = qkv_s[:, :att_w].reshape(t_new, bs, att_heads, 2, hd_att)
        qpp = jnp.einsum("tbhmd,hH,mM->bhmdHMt", q_s, eye_h, eye_m).reshape(bs, att_w, att_heads * 2 * t_new)
        qpp = jnp.pad(qpp, ((0, 0), (0, 0), (0, ncol - qpp.shape[2])))
        tm_to_seq = lambda a: jnp.pad(a.reshape(t_new, bs, -1).transpose(1, 0, 2),
                                      ((0, 0), (0, SUBLANES - t_new), (0, 0)))
        att_s = _attn_sample(page_table, lam, qpp, tm_to_seq(k_s), tm_to_seq(v_s), attn_subln[l], ck, cv, layer=l,
                             pages=pages, t_new=t_new, heads=att_heads, lam_init=lam_i)
        att_s = att_s.transpose(2, 0, 1, 3).reshape(t_new * bs, att_w)

        cy_p, nb_p = _conv_prompt(u_p, conv_dw[l], conv_dw_b[l], conv_ln_g[l], conv_ln_b[l], batch=batch,
                                  tt=conv_tt, rc=conv_rc)
        cy_s, nb_s = _conv_sample(u_s.reshape(t_new, bs, 2 * conv_ch), state_conv[l].transpose(1, 0, 2),
                                  conv_dw[l], conv_dw_b[l], conv_ln_g[l], conv_ln_b[l])
        cy_s = cy_s.reshape(t_new * bs, conv_ch)

        c_p3 = c_p.reshape(batch, lp, -1)
        prev_p = jnp.concatenate([jnp.zeros((batch, 1, c_p.shape[1]), F32), c_p3[:, :-1]], axis=1)
        prev_s = jnp.concatenate([state_shift[l], c_s[:-bs]], axis=0)
        pk_p = _rw_prep(c_p, prev_p.reshape(batch * lp, -1), pk0_p, rwp, rw_w=rw_w, hd=rw_hd, tm=tm)
        pk_s = _rw_prep(c_s, prev_s, pk0_s, rwp, rw_w=rw_w, hd=rw_hd, tm=tm)
        if l == 0:
            pk0_p, pk0_s = pk_p, pk_s
        ry_p, S_p = _wkv(pk_p.reshape(batch, lp, 7 * rw_w), jnp.zeros((batch, rw_heads, rw_hd, rw_hd), F32),
                         rw_rk[l], rw_lnx_g[l], rw_lnx_b[l], chunk=chunk_p, sb=batch)
        pk_s3 = jnp.pad(pk_s.reshape(t_new, bs, -1).transpose(1, 0, 2), ((0, 0), (0, chunk_s - t_new), (0, 0)))
        ry_s, S_s = _wkv(pk_s3, state_wkv[l], rw_rk[l], rw_lnx_g[l], rw_lnx_b[l], chunk=chunk_s,
                         sb=math.gcd(bs, SUBLANES))
        ry_s = ry_s[:, :t_new].transpose(1, 0, 2).reshape(t_new * bs, rw_w)

        x2_p, h2_p, qp_p = _out_proj(xp, att_p, cy_p, ry_p.reshape(batch * lp, rw_w), wo_bf, norm_ffn[l], wq_bf, tm=tm)
        x2_s, h2_s, qp_s = _out_proj(xs, att_s, cy_s, ry_s, wo_bf, norm_ffn[l], wq_bf, tm=tm)
        xp = _peer(x2_p, h2_p, qp_p, k1_bf, k2_bf, u_bf, v_bf, tn=512, ec=peer_ec)
        xs = _peer(x2_s, h2_s, qp_s, k1_bf, k2_bf, u_bf, v_bf, tn=512, ec=peer_ec)

        seq_major = lambda a: a.reshape(t_new, bs, -1).transpose(1, 0, 2)
        outs["kp"].append(k_p.reshape(batch, lp, att_heads, att_vd))
        outs["vp"].append(v_p.reshape(batch, lp, att_heads, att_vd))
        outs["cp"].append(nb_p)
        outs["sp"].append(c_p3[:, -1])
        outs["Sp"].append(S_p)
        outs["ks"].append(seq_major(k_s).reshape(bs, t_new, att_heads, att_vd))
        outs["vs"].append(seq_major(v_s).reshape(bs, t_new, att_heads, att_vd))
        outs["cs"].append(nb_s.transpose(1, 0, 2))
        outs["ss"].append(c_s[-bs:])
        outs["Ss"].append(S_s)

    y_prompt = _rmsnorm(xp, norm_final, tm=tm).reshape(batch, lp, d)[:, n_meta:]
    y_sample = _rmsnorm(xs, norm_final, tm=tm).reshape(t_new, bs, d).transpose(1, 0, 2)
    st = lambda name: jnp.stack(outs[name])
    return (y_prompt, y_sample, st("kp"), st("vp"), st("cp"), st("sp"), st("Sp"),
            st("ks"), st("vs"), st("cs"), st("ss"), st("Ss"))
```

```python
import functools
import math

import jax
import jax.numpy as jnp
from jax import lax
from jax.experimental import pallas as pl
from jax.experimental.pallas import tpu as pltpu

F32 = jnp.float32
BF16 = jnp.bfloat16
HI = lax.Precision.HIGHEST

EPS = 1e-6
RW_LNX_EPS = 64e-5
CONV_LN_EPS = 1e-5
PEER_TOPK = 16
LANES = 128
SUBLANES = 8
VMEM_LIMIT = 56 * 1024 * 1024
NEG_BIG = -1e30


def _lambda_init(l):
    return 0.8 - 0.6 * math.exp(-0.3 * l)


def _cparams(sem):
    return pltpu.CompilerParams(dimension_semantics=sem, vmem_limit_bytes=VMEM_LIMIT)


def _dot(a, b, precision=None):
    return jnp.dot(a, b, preferred_element_type=F32, precision=precision)


def _dot_nt(a, b, precision=None):
    return lax.dot_general(a, b, (((1,), (1,)), ((), ())), preferred_element_type=F32, precision=precision)


def _dot_tn(a, b, precision=None):
    return lax.dot_general(a, b, (((0,), (0,)), ((), ())), preferred_element_type=F32, precision=precision)


def _split2(x):
    hi = x.astype(BF16)
    return hi, (x - hi.astype(F32)).astype(BF16)


def _full(shape):
    nd = len(shape)
    return pl.BlockSpec(shape, lambda *_: (0,) * nd)


def _norm_proj_kernel(x_ref, g_ref, w_ref, qkv_ref, k_ref, v_ref, u_ref, c_ref, *, att_w, conv_w2, q_scale):
    x = x_ref[...]
    h = x * lax.rsqrt(jnp.mean(x * x, axis=-1, keepdims=True) + EPS) * g_ref[...]
    hb = h.astype(BF16)
    o1, o2, o3 = att_w, 2 * att_w, 3 * att_w
    o4 = o3 + conv_w2
    q = _dot(hb, w_ref[:, 0:o1])
    k = _dot(hb, w_ref[:, o1:o2])
    v = _dot(hb, w_ref[:, o2:o3])
    qkv_ref[:, 0:o1] = (q * q_scale).astype(BF16)
    qkv_ref[:, o1:o2] = k.astype(BF16)
    qkv_ref[:, o2:o3] = v.astype(BF16)
    k_ref[...] = k
    v_ref[...] = v
    u_ref[...] = _dot(hb, w_ref[:, o3:o4])
    c_ref[...] = _dot(hb, w_ref[:, o4:])


def _norm_proj(x, g, w_bf, *, att_w, conv_w2, q_scale, tm):
    n, d = x.shape
    d_in = w_bf.shape[1]
    shift_w = d_in - 3 * att_w - conv_w2
    tm = min(tm, n)
    row = lambda w: pl.BlockSpec((tm, w), lambda i: (i, 0))
    return pl.pallas_call(
        functools.partial(_norm_proj_kernel, att_w=att_w, conv_w2=conv_w2, q_scale=q_scale),
        grid=(pl.cdiv(n, tm),),
        in_specs=[row(d), _full((1, d)), _full((d, d_in))],
        out_specs=[row(3 * att_w), row(att_w), row(att_w), row(conv_w2), row(shift_w)],
        out_shape=[jax.ShapeDtypeStruct((n, 3 * att_w), BF16), jax.ShapeDtypeStruct((n, att_w), F32),
                   jax.ShapeDtypeStruct((n, att_w), F32), jax.ShapeDtypeStruct((n, conv_w2), F32),
                   jax.ShapeDtypeStruct((n, shift_w), F32)],
        compiler_params=_cparams(("parallel",)),
        name="norm_proj",
    )(x, g.reshape(1, d), w_bf)


def _softmax_step(carry, q1, q2, kt, vt, hd, mask):
    m1, l1, a1, m2, l2, a2 = carry
    out = []
    for (m, l, a, q, ks) in ((m1, l1, a1, q1, kt[:, :hd]), (m2, l2, a2, q2, kt[:, hd:])):
        s = _dot_nt(q, ks)
        if mask is not None:
            s = jnp.where(mask, s, NEG_BIG)
        mn = jnp.maximum(m, jnp.max(s, axis=-1, keepdims=True))
        alpha = jnp.exp(m - mn)
        p = jnp.exp(s - mn)
        l = alpha * l + jnp.sum(p, axis=-1, keepdims=True)
        a = alpha * a + _dot(p.astype(BF16), vt)
        out += [mn, l, a]
    return tuple(out)


def _attn_prompt_kernel(lam_ref, q_ref, k_ref, v_ref, g_ref, gcol_ref, o_ref, vt_sc, acc_sc, *, n_meta, tq, tk, hd,
                        lam_init):
    i = pl.program_id(2)
    lam = lam_ref[0]
    vd = v_ref.shape[1]
    n_real = q_ref.shape[0] - n_meta

    @pl.when(i == 0)
    def _():
        q = q_ref[0:n_meta, :]
        z = jnp.zeros((n_meta, 1), F32)
        za = jnp.zeros((n_meta, vd), F32)
        r = lax.broadcasted_iota(jnp.int32, (n_meta, n_meta), 0)
        c = lax.broadcasted_iota(jnp.int32, (n_meta, n_meta), 1)
        m1, l1, a1, m2, l2, a2 = _softmax_step((z + NEG_BIG, z, za, z + NEG_BIG, z, za), q[:, :hd], q[:, hd:],
                                               k_ref[0:n_meta, :], v_ref[0:n_meta, :], hd, c <= r)
        o = a1 / l1 - lam * (a2 / l2)
        o = o * lax.rsqrt(jnp.mean(o * o, axis=-1, keepdims=True) + EPS) * g_ref[...]
        o_ref[0:n_meta, :] = o * (1.0 - lam_init)

        def transpose_tile(j, carry):
            rows = pl.ds(pl.multiple_of(n_meta + j * LANES, 16), LANES)
            vt_sc[:, pl.ds(pl.multiple_of(j * LANES, LANES), LANES)] = v_ref[rows, :].astype(F32).T.astype(BF16)
            return carry

        lax.fori_loop(0, n_real // LANES, transpose_tile, 0)

    @pl.when(i > 0)
    def _():
        r0 = pl.multiple_of(n_meta + (i - 1) * tq, 16)
        q = q_ref[pl.ds(r0, tq), :]
        qm = (q[:, :hd], q[:, hd:])
        k_meta = k_ref[0:n_meta, :]
        v_meta = v_ref[0:n_meta, :]
        stats = []
        for mi in range(2):
            st = _dot_nt(k_meta[:, mi * hd:(mi + 1) * hd], qm[mi])
            m = jnp.max(st, axis=0, keepdims=True)
            p = jnp.exp(st - m)
            acc_sc[mi] = _dot_tn(v_meta, p.astype(BF16))
            stats += [m, jnp.sum(p, axis=0, keepdims=True)]

        def step(j, carry, masked):
            k0 = j * tk
            kt = k_ref[pl.ds(pl.multiple_of(n_meta + k0, 16), tk), :]
            vt = vt_sc[:, pl.ds(pl.multiple_of(k0, LANES), tk)]
            out = []
            for mi in range(2):
                m, l = carry[2 * mi], carry[2 * mi + 1]
                st = _dot_nt(kt[:, mi * hd:(mi + 1) * hd], qm[mi])
                if masked:
                    kr = lax.broadcasted_iota(jnp.int32, (tk, tq), 0) + k0
                    qc = lax.broadcasted_iota(jnp.int32, (tk, tq), 1) + (i - 1) * tq
                    st = jnp.where(kr <= qc, st, NEG_BIG)
                mn = jnp.maximum(m, jnp.max(st, axis=0, keepdims=True))
                alpha = jnp.exp(m - mn)
                p = jnp.exp(st - mn)
                acc_sc[mi] = alpha * acc_sc[mi] + _dot(vt, p.astype(BF16))
                out += [mn, alpha * l + jnp.sum(p, axis=0, keepdims=True)]
            return tuple(out)

        n_full = lax.div(i - 1, tk // tq)
        carry = lax.fori_loop(0, n_full, lambda j, c: step(j, c, False), tuple(stats))
        m1, l1, m2, l2 = step(n_full, carry, True)
        o = acc_sc[0] / l1 - lam * (acc_sc[1] / l2)
        o = o * lax.rsqrt(jnp.mean(o * o, axis=0, keepdims=True) + EPS) * gcol_ref[...]
        o_ref[pl.ds(r0, tq), :] = (o * (1.0 - lam_init)).T


def _attn_prompt(qkv_bf, lam, subln, *, batch, heads, n_meta, tq, lam_init):
    n, w3 = qkv_bf.shape
    lp = n // batch
    vd = w3 // 3 // heads
    n_real = lp - n_meta
    tk = math.gcd(n_real, 4 * tq)
    assert vd == LANES and n_real % tq == 0 and tq % LANES == 0 and tk % tq == 0
    nq = 1 + n_real // tq
    seq = lambda off: pl.BlockSpec((lp, vd), lambda b, h, i: (b, off + h))
    return pl.pallas_call(
        functools.partial(_attn_prompt_kernel, n_meta=n_meta, tq=tq, tk=tk, hd=vd // 2, lam_init=lam_init),
        grid=(batch, heads, nq),
        in_specs=[pl.BlockSpec(memory_space=pltpu.SMEM), seq(0), seq(heads), seq(2 * heads),
                  pl.BlockSpec((1, vd), lambda b, h, i: (0, 0)), pl.BlockSpec((vd, 1), lambda b, h, i: (0, 0))],
        out_specs=seq(0),
        out_shape=jax.ShapeDtypeStruct((n, heads * vd), F32),
        scratch_shapes=[pltpu.VMEM((vd, lp - n_meta), BF16), pltpu.VMEM((2, vd, tq), F32)],
        compiler_params=_cparams(("parallel", "parallel", "arbitrary")),
        name="attn_prompt",
    )(lam, qkv_bf, qkv_bf, qkv_bf, subln.reshape(1, vd), subln.reshape(vd, 1))


def _attn_sample_kernel(pt_ref, lam_ref, q_ref, kn_ref, vn_ref, g_ref, *refs, pages, t_new, heads, lam_init):
    k_refs = refs[:pages]
    v_refs = refs[pages:2 * pages]
    o_ref = refs[2 * pages]
    m_sc, l_sc, acc_sc = refs[2 * pages + 1:]
    s_idx = pl.program_id(1)
    ncol = q_ref.shape[2]
    vd = o_ref.shape[3]

    @pl.when(s_idx == 0)
    def _():
        m_sc[...] = jnp.full(m_sc.shape, NEG_BIG, F32)
        l_sc[...] = jnp.zeros(l_sc.shape, F32)
        acc_sc[...] = jnp.zeros(acc_sc.shape, F32)

    eye = (lax.broadcasted_iota(jnp.int32, (ncol, ncol), 0) == lax.broadcasted_iota(jnp.int32, (ncol, ncol), 1))

    def to_col(row):
        return jnp.sum(jnp.where(eye, jnp.broadcast_to(row, (ncol, ncol)), 0.0), axis=1, keepdims=True)

    def update(k_heads, v_heads, mask):
        st = _dot(k_heads[0], qpp[0:vd, :])
        for h in range(1, heads):
            st = st + _dot(k_heads[h], qpp[vd * h:vd * (h + 1), :])
        if mask is not None:
            st = jnp.where(mask, st, NEG_BIG)
        m_old = m_sc[...]
        m_new = jnp.maximum(m_old, jnp.max(st, axis=0, keepdims=True))
        alpha = jnp.exp(m_old - m_new)
        p = jnp.exp(st - m_new)
        l_sc[...] = alpha * l_sc[...] + jnp.sum(p, axis=0, keepdims=True)
        pb = p.astype(BF16)
        a_col = to_col(alpha)
        for h in range(heads):
            acc_sc[:, vd * h:vd * (h + 1)] = a_col * acc_sc[:, vd * h:vd * (h + 1)] + _dot_tn(pb, v_heads[h])
        m_sc[...] = m_new

    qpp = q_ref[0]
    cat = lambda page_refs, h: jnp.concatenate([r[:, h, :] for r in page_refs], axis=0).astype(BF16)
    update([cat(k_refs, h) for h in range(heads)], [cat(v_refs, h) for h in range(heads)], None)

    @pl.when(s_idx == pl.num_programs(1) - 1)
    def _():
        lam = lam_ref[0]
        kn = kn_ref[0].astype(BF16)
        vn = vn_ref[0].astype(BF16)
        rows = kn.shape[0]
        r = lax.broadcasted_iota(jnp.int32, (rows, ncol), 0)
        c = lax.broadcasted_iota(jnp.int32, (rows, ncol), 1)
        mask = (r < t_new) & (r <= lax.rem(c, t_new))
        update([kn[:, vd * h:vd * (h + 1)] for h in range(heads)],
               [vn[:, vd * h:vd * (h + 1)] for h in range(heads)], mask)
        out = acc_sc[...] / to_col(l_sc[...])
        for h in range(heads):
            blk = out[2 * t_new * h:2 * t_new * (h + 1), vd * h:vd * (h + 1)]
            o = blk[0:t_new] - lam * blk[t_new:2 * t_new]
            o = o * lax.rsqrt(jnp.mean(o * o, axis=-1, keepdims=True) + EPS) * g_ref[...]
            o_ref[0, h] = o * (1.0 - lam_init)


def _attn_sample(page_table, lam, qpp, k_new, v_new, subln, cache_k, cache_v, *, layer, pages, t_new, heads,
                 lam_init):
    bs, n_pages = page_table.shape
    page, vd = cache_k.shape[2], cache_k.shape[4]
    kw = heads * vd
    ncol = qpp.shape[2]
    steps = n_pages // pages

    def page_spec(p):
        return pl.BlockSpec((None, None, page, heads, vd), lambda b, s, pt: (layer, pt[b, s * pages + p], 0, 0, 0))

    per_seq = lambda shape: pl.BlockSpec((1,) + shape, lambda b, s, pt: (b,) + (0,) * len(shape))
    grid_spec = pltpu.PrefetchScalarGridSpec(
        num_scalar_prefetch=1,
        grid=(bs, steps),
        in_specs=[pl.BlockSpec(memory_space=pltpu.SMEM), per_seq(qpp.shape[1:]), per_seq(k_new.shape[1:]),
                  per_seq(v_new.shape[1:]), pl.BlockSpec((1, vd), lambda b, s, pt: (0, 0))]
                 + [page_spec(p) for p in range(pages)] * 2,
        out_specs=per_seq((heads, t_new, vd)),
        scratch_shapes=[pltpu.VMEM((1, ncol), F32), pltpu.VMEM((1, ncol), F32), pltpu.VMEM((ncol, kw), F32)],
    )
    return pl.pallas_call(
        functools.partial(_attn_sample_kernel, pages=pages, t_new=t_new, heads=heads, lam_init=lam_init),
        grid_spec=grid_spec,
        out_shape=jax.ShapeDtypeStruct((bs, heads, t_new, vd), F32),
        compiler_params=_cparams(("parallel", "arbitrary")),
        name="attn_sample",
    )(page_table, lam, qpp, k_new, v_new, subln.reshape(1, vd), *([cache_k] * pages), *([cache_v] * pages))


def _conv_post(acc, g, b):
    mu = jnp.mean(acc, axis=-1, keepdims=True)
    d = acc - mu
    var = jnp.mean(d * d, axis=-1, keepdims=True)
    y = d * lax.rsqrt(var + CONV_LN_EPS) * g + b
    return y * jax.nn.sigmoid(y)


def _conv_prompt_kernel(u_ref, dw_ref, b_ref, g_ref, bb_ref, y_ref, nb_ref, ext, *, tt, rc, taps, halo):
    i = pl.program_id(1)
    ch = y_ref.shape[1]
    lead = halo - (taps - 1)

    @pl.when(i == 0)
    def _():
        ext[0:halo, :] = jnp.zeros((halo, ch), F32)

    @pl.when(i > 0)
    def _():
        ext[0:halo, :] = ext[tt:tt + halo, :]

    u = u_ref[...]
    ext[halo:halo + tt, :] = u[:, :ch] * jax.nn.sigmoid(u[:, ch:])
    for c0 in range(0, tt, rc):
        acc = jnp.broadcast_to(b_ref[...], (rc, ch))
        for j in range(taps):
            acc = acc + ext[c0 + lead + j:c0 + lead + j + rc, :] * dw_ref[j:j + 1, :]
        y_ref[c0:c0 + rc, :] = _conv_post(acc, g_ref[...], bb_ref[...])

    @pl.when(i == pl.num_programs(1) - 1)
    def _():
        nb_ref[0] = ext[tt + lead:tt + halo, :]


def _conv_prompt(u, dw, b, g, bb, *, batch, tt, rc):
    n, w2 = u.shape
    ch = w2 // 2
    taps = dw.shape[0]
    halo = 32
    lp = n // batch
    nt = lp // tt
    vec = lambda: pl.BlockSpec((1, ch), lambda bi, i: (0, 0))
    return pl.pallas_call(
        functools.partial(_conv_prompt_kernel, tt=tt, rc=rc, taps=taps, halo=halo),
        grid=(batch, nt),
        in_specs=[pl.BlockSpec((tt, w2), lambda bi, i: (bi * nt + i, 0)),
                  pl.BlockSpec((taps, ch), lambda bi, i: (0, 0)), vec(), vec(), vec()],
        out_specs=[pl.BlockSpec((tt, ch), lambda bi, i: (bi * nt + i, 0)),
                   pl.BlockSpec((1, taps - 1, ch), lambda bi, i: (bi, 0, 0))],
        out_shape=[jax.ShapeDtypeStruct((n, ch), F32), jax.ShapeDtypeStruct((batch, taps - 1, ch), F32)],
        scratch_shapes=[pltpu.VMEM((halo + tt, ch), F32)],
        compiler_params=_cparams(("parallel", "arbitrary")),
        name="conv_prompt",
    )(u, dw, b.reshape(1, ch), g.reshape(1, ch), bb.reshape(1, ch))


def _conv_sample_kernel(u_ref, buf_ref, dw_ref, b_ref, g_ref, bb_ref, y_ref, nb_ref):
    t_new, bs, ch = y_ref.shape
    taps = dw_ref.shape[0]
    full = [buf_ref[j] for j in range(taps - 1)]
    for t in range(t_new):
        u = u_ref[t]
        full.append(u[:, :ch] * jax.nn.sigmoid(u[:, ch:]))
    for t in range(t_new):
        acc = jnp.broadcast_to(b_ref[...], (bs, ch))
        for j in range(taps):
            acc = acc + full[t + j] * dw_ref[j:j + 1, :]
        y_ref[t] = _conv_post(acc, g_ref[...], bb_ref[...])
    for j in range(taps - 1):
        nb_ref[j] = full[t_new + j]


def _conv_sample(u_tm, buf_tm, dw, b, g, bb):
    t_new, bs, w2 = u_tm.shape
    ch = w2 // 2
    taps = dw.shape[0]
    return pl.pallas_call(
        _conv_sample_kernel,
        in_specs=[_full(u_tm.shape), _full(buf_tm.shape), _full(dw.shape), _full((1, ch)), _full((1, ch)),
                  _full((1, ch))],
        out_specs=[_full((t_new, bs, ch)), _full((taps - 1, bs, ch))],
        out_shape=[jax.ShapeDtypeStruct((t_new, bs, ch), F32), jax.ShapeDtypeStruct((taps - 1, bs, ch), F32)],
        grid=(1,),
        name="conv_sample",
    )(u_tm, buf_tm, dw, b.reshape(1, ch), g.reshape(1, ch), bb.reshape(1, ch))


def _head_ones(w, hd):
    r = lax.broadcasted_iota(jnp.int32, (w, w), 0)
    c = lax.broadcasted_iota(jnp.int32, (w, w), 1)
    shift = hd.bit_length() - 1
    return (lax.shift_right_logical(r, shift) == lax.shift_right_logical(c, shift)).astype(F32)


def _rw_prep_kernel(*refs, rw_w, hd, mix_v):
    if mix_v:
        (c_ref, p_ref, vf_ref, mu_ref, w0_ref, w2_ref, a0_ref, a2_ref, g2_ref, kk_ref, ka_ref,
         v0_ref, v1_ref, v2_ref, o_ref) = refs
    else:
        (c_ref, p_ref, mu_ref, w0_ref, w2_ref, a0_ref, a2_ref, g2_ref, kk_ref, ka_ref, o_ref) = refs
    cols = c_ref[...]
    xsh = cols + (p_ref[...] - cols) * mu_ref[...]
    r = xsh[:, 0:rw_w]
    k = xsh[:, rw_w:2 * rw_w]
    v = xsh[:, 2 * rw_w:3 * rw_w]
    lora = xsh[:, 3 * rw_w:]
    w_log = -jax.nn.softplus(-(w0_ref[...] + _dot(jnp.tanh(lora), w2_ref[...], HI))) - 0.5
    a = jax.nn.sigmoid(a0_ref[...] + _dot(lora, a2_ref[...], HI))
    g = _dot(jax.nn.sigmoid(lora), g2_ref[...], HI)
    if mix_v:
        mixw = jax.nn.sigmoid(v0_ref[...] + _dot(_dot(v, v1_ref[...], HI), v2_ref[...], HI))
        v = v + (vf_ref[...] - v) * mixw
    kk = k * kk_ref[...]
    n2 = _dot(kk * kk, _head_ones(rw_w, hd), HI)
    kk = kk / jnp.maximum(jnp.sqrt(n2), 1e-12)
    k2 = k * (1.0 + (a - 1.0) * ka_ref[...])
    for idx, val in enumerate((r, -jnp.exp(w_log), k2, v, kk, a, g)):
        o_ref[:, idx * rw_w:(idx + 1) * rw_w] = val


def _rw_prep(cols, prev, vfirst_src, p, *, rw_w, hd, tm):
    n, sw = cols.shape
    tm = min(tm, n)
    lw = sw - 3 * rw_w
    mix_v = vfirst_src is not None
    row = lambda w: pl.BlockSpec((tm, w), lambda i: (i, 0))
    vec = lambda a: a.reshape(1, -1)
    args = [cols, prev]
    specs = [row(sw), row(sw)]
    if mix_v:
        args.append(vfirst_src)
        specs.append(pl.BlockSpec((tm, rw_w), lambda i: (i, 3)))
    consts = [vec(p["rw_mu"]), vec(p["rw_w0"]), p["w2p"], vec(p["rw_a0"]), p["a2p"], p["g2p"], vec(p["rw_kk"]),
              vec(p["rw_ka"])]
    if mix_v:
        consts += [vec(p["rw_v0"]), p["v1p"], p["v2p"]]
    args += consts
    specs += [_full(c.shape) for c in consts]
    del lw
    return pl.pallas_call(
        functools.partial(_rw_prep_kernel, rw_w=rw_w, hd=hd, mix_v=mix_v),
        grid=(pl.cdiv(n, tm),),
        in_specs=specs,
        out_specs=row(7 * rw_w),
        out_shape=jax.ShapeDtypeStruct((n, 7 * rw_w), F32),
        compiler_params=_cparams(("parallel",)),
        name="rw_prep",
    )(*args)


def _wkv_kernel(pk_ref, s0_ref, rk_ref, lg_ref, lb_ref, y_ref, so_ref, s_sc, *, chunk, heads, hd, seq_len):
    j = pl.program_id(1)
    sb = pk_ref.shape[0]
    rw_w = heads * hd
    c2 = 2 * chunk

    @pl.when(j == 0)
    def _():
        s_sc[...] = s0_ref[...]

    row = lax.broadcasted_iota(jnp.int32, (chunk, 1), 0)
    valid = (row + j * chunk) < seq_len
    r_i = lax.broadcasted_iota(jnp.int32, (chunk, chunk), 0)
    c_i = lax.broadcasted_iota(jnp.int32, (chunk, chunk), 1)
    tril_b = jnp.where(c_i <= r_i, 1.0, 0.0).astype(BF16)
    tril3 = jnp.concatenate([tril_b, tril_b, tril_b], axis=1)
    eye = (c_i == r_i).astype(F32)
    rr = lax.broadcasted_iota(jnp.int32, (c2, c2), 0)
    cc = lax.broadcasted_iota(jnp.int32, (c2, c2), 1)
    keep = (cc & (chunk - 1)) <= jnp.where(rr < chunk, rr - 1, rr - chunk)
    n_sq = chunk.bit_length() - 2
    lhs3 = lambda hi, lo, axis: jnp.concatenate([hi, lo, hi], axis=axis)
    rhs3 = lambda hi, lo, axis: jnp.concatenate([hi, hi, lo], axis=axis)

    def mm(a, b):
        return _dot(lhs3(*_split2(a), 1), rhs3(*_split2(b), 0))

    for s in range(sb):
        blk = jnp.where(valid, pk_ref[s], 0.0)
        r_a, nlw_a, k_a, v_a, kk_a, a_a, g_a = (blk[:, t * rw_w:(t + 1) * rw_w] for t in range(7))
        n1 = nlw_a.astype(BF16)
        res = nlw_a - n1.astype(F32)
        n2 = res.astype(BF16)
        n3 = (res - n2.astype(F32)).astype(BF16)
        cum = _dot(tril3, jnp.concatenate([n1, n2, n3], axis=0))
        p_inc = jnp.exp(cum)
        p_inv = jnp.exp(-cum)
        ar_a = jnp.concatenate([-kk_a * jnp.exp(cum - nlw_a), r_a * p_inc], axis=0)
        bk_a = jnp.concatenate([kk_a * a_a * p_inv, k_a * p_inv], axis=0)
        arh_a, arl_a = _split2(ar_a)
        bkh_a, bkl_a = _split2(bk_a)
        bonus_a = r_a * k_a * rk_ref[...]
        outs = []
        for h in range(heads):
            hs = slice(h * hd, (h + 1) * hd)
            v = v_a[:, hs]
            ar3 = lhs3(arh_a[:, hs], arl_a[:, hs], 1)
            bkh, bkl = bkh_a[:, hs], bkl_a[:, hs]
            m = jnp.where(keep, _dot_nt(ar3, rhs3(bkh, bkl, 1)), 0.0)
            s_prev = s_sc[s, h]
            ars = _dot_nt(ar3, rhs3(*_split2(s_prev), 1))
            x = m[:chunk, :chunk]
            inv = eye + x
            for _ in range(n_sq):
                x = mm(x, x)
                inv = inv + mm(inv, x)
            u = mm(inv, ars[:chunk] + mm(m[:chunk, chunk:], v))
            y = ars[chunk:] + mm(m[chunk:, :chunk], u) + mm(m[chunk:, chunk:], v)
            ds = (_dot_tn(lhs3(*_split2(u), 0), rhs3(bkh[:chunk], bkl[:chunk], 0))
                  + _dot_tn(lhs3(*_split2(v), 0), rhs3(bkh[chunk:], bkl[chunk:], 0)))
            s_sc[s, h] = (s_prev + ds) * p_inc[chunk - 1:chunk, hs]
            mu = jnp.mean(y, axis=-1, keepdims=True)
            d = y - mu
            var = jnp.mean(d * d, axis=-1, keepdims=True)
            yn = d * lax.rsqrt(var + RW_LNX_EPS) * lg_ref[:, hs] + lb_ref[:, hs]
            bonus = jnp.sum(bonus_a[:, hs], axis=-1, keepdims=True) * v
            outs.append((yn + bonus) * g_a[:, hs])
        y_ref[s] = jnp.concatenate(outs, axis=-1)

    @pl.when(j == pl.num_programs(1) - 1)
    def _():
        so_ref[...] = s_sc[...]


def _wkv(pk, s0, rk, lg, lb, *, chunk, sb):
    n_seq, seq_len, w7 = pk.shape
    heads, hd = s0.shape[1], s0.shape[2]
    rw_w = heads * hd
    vec = lambda: pl.BlockSpec((1, rw_w), lambda i, j: (0, 0))
    return pl.pallas_call(
        functools.partial(_wkv_kernel, chunk=chunk, heads=heads, hd=hd, seq_len=seq_len),
        grid=(n_seq // sb, pl.cdiv(seq_len, chunk)),
        in_specs=[pl.BlockSpec((sb, chunk, w7), lambda i, j: (i, j, 0)),
                  pl.BlockSpec((sb, heads, hd, hd), lambda i, j: (i, 0, 0, 0)), vec(), vec(), vec()],
        out_specs=[pl.BlockSpec((sb, chunk, rw_w), lambda i, j: (i, j, 0)),
                   pl.BlockSpec((sb, heads, hd, hd), lambda i, j: (i, 0, 0, 0))],
        out_shape=[jax.ShapeDtypeStruct((n_seq, seq_len, rw_w), F32),
                   jax.ShapeDtypeStruct((n_seq, heads, hd, hd), F32)],
        scratch_shapes=[pltpu.VMEM((sb, heads, hd, hd), F32)],
        compiler_params=_cparams(("parallel", "arbitrary")),
        name="wkv",
    )(pk, s0, rk.reshape(1, rw_w), lg.reshape(1, rw_w), lb.reshape(1, rw_w))


def _out_proj_kernel(x_ref, a_ref, c_ref, r_ref, wo_ref, g_ref, wq_ref, x2_ref, h2_ref, q_ref):
    aw, cw = a_ref.shape[1], c_ref.shape[1]
    x2 = (x_ref[...] + _dot(a_ref[...].astype(BF16), wo_ref[0:aw, :])
          + _dot(c_ref[...].astype(BF16), wo_ref[aw:aw + cw, :])
          + _dot(r_ref[...].astype(BF16), wo_ref[aw + cw:, :]))
    x2_ref[...] = x2
    h2 = (x2 * lax.rsqrt(jnp.mean(x2 * x2, axis=-1, keepdims=True) + EPS) * g_ref[...]).astype(BF16)
    h2_ref[...] = h2
    q_ref[...] = _dot(h2, wq_ref[...]).astype(BF16)


def _out_proj(x, att, cy, ry, wo_bf, g, wq_bf, *, tm):
    n, d = x.shape
    tm = min(tm, n)
    qw = wq_bf.shape[1]
    row = lambda w: pl.BlockSpec((tm, w), lambda i: (i, 0))
    return pl.pallas_call(
        _out_proj_kernel,
        grid=(pl.cdiv(n, tm),),
        in_specs=[row(d), row(att.shape[1]), row(cy.shape[1]), row(ry.shape[1]), _full(wo_bf.shape), _full((1, d)),
                  _full(wq_bf.shape)],
        out_specs=[row(d), row(d), row(qw)],
        out_shape=[jax.ShapeDtypeStruct((n, d), F32), jax.ShapeDtypeStruct((n, d), BF16),
                   jax.ShapeDtypeStruct((n, qw), BF16)],
        compiler_params=_cparams(("parallel",)),
        name="out_proj",
    )(x, att, cy, ry, wo_bf, g.reshape(1, d), wq_bf)


def _vmax(a, b):
    if a is None:
        return b
    if b is None:
        return a
    return jnp.maximum(a, b)


def _cmpx(xs, i, j):
    a, b = xs[i], xs[j]
    if a is None or b is None:
        xs[i], xs[j] = _vmax(a, b), None
    else:
        xs[i], xs[j] = jnp.maximum(a, b), jnp.minimum(a, b)


def _bitonic_merge_desc(xs):
    n = len(xs)
    stride = n // 2
    while stride >= 1:
        for i in range(n):
            if (i & stride) == 0:
                _cmpx(xs, i, i + stride)
        stride //= 2


def _sort_desc(xs):
    n = len(xs)
    size = 2
    while size <= n:
        for base in range(0, n, size):
            half = size // 2
            for t in range(half):
                _cmpx(xs, base + t, base + size - 1 - t)
            for sub in (base, base + half):
                seg = xs[sub:sub + half]
                _bitonic_merge_desc(seg)
                xs[sub:sub + half] = seg
        size *= 2


def _merge_top(a, b):
    n = len(a)
    z = [_vmax(a[t], b[n - 1 - t]) for t in range(n)]
    _bitonic_merge_desc(z)
    return z


def _top_sorted(x_rows):
    xs = list(x_rows)
    _sort_desc(xs)
    shift = SUBLANES // 2
    while shift >= 1:
        xs = _merge_top(xs, [pltpu.roll(t, shift, 0) for t in xs])
        shift //= 2
    return xs


def _peer_kernel(x2_ref, h2_ref, q_ref, k1_ref, k2_ref, u_ref, v_ref, o_ref,
                 s1_sc, s2_sc, e1_sc, e2_sc, thr_sc, a_sc, w_sc, acc_sc, *, heads, topk, nkeys):
    c = pl.program_id(1)
    tn = h2_ref.shape[0]
    groups = tn // LANES
    rows_per_chunk = u_ref.shape[0] // nkeys
    kslabs = nkeys // SUBLANES
    half = k1_ref.shape[2]

    @pl.when(c == 0)
    def _():
        acc_sc[...] = jnp.zeros(acc_sc.shape, F32)
        sub = lax.broadcasted_iota(jnp.int32, (SUBLANES, LANES), 0)

        def group(gi, carry):
            r0 = pl.multiple_of(gi * LANES, LANES)
            lanes = pl.ds(r0, LANES)
            qg = q_ref[pl.ds(r0, LANES), :]
            t1 = [None] * topk
            t2 = [None] * topk
            for h in range(heads):
                for side, (k_ref, s_sc, e_sc, tl) in enumerate(((k1_ref, s1_sc, e1_sc, t1), (k2_ref, s2_sc, e2_sc, t2))):
                    c0 = (2 * h + side) * half
                    st = _dot_nt(k_ref[h], qg[:, c0:c0 + half])
                    s_sc[h, :, lanes] = st
                    top = _top_sorted([st[SUBLANES * r:SUBLANES * (r + 1), :] for r in range(kslabs)])[:topk]
                    e_sc[h, :, lanes] = jnp.exp(st - top[0][0:1, :])
                    for t in range(topk):
                        tl[t] = top[t] if h == 0 else jnp.where(sub == h, top[t], tl[t])
            best = None
            for a_i in range(topk):
                nb = topk // (a_i + 1)
                lst = [t1[a_i] + t2[b_i] for b_i in range(nb)] + [None] * (topk - nb)
                best = lst if best is None else _merge_top(best, lst)
            z = jnp.zeros((SUBLANES, LANES), F32)
            for t in range(topk):
                z = z + jnp.exp(best[t] - best[0])
            inv_z = 1.0 / z
            thr_sc[:, lanes] = best[topk - 1]
            for h in range(heads):
                e2_sc[h, :, lanes] = e2_sc[h, :, lanes] * inv_z[h:h + 1, :]
            return carry

        lax.fori_loop(0, groups, group, 0)

    a_sc[...] = _dot_nt(u_ref[...], h2_ref[...])
    i1_base = pl.multiple_of(c * rows_per_chunk, SUBLANES)

    def dense_group(gi, carry):
        lanes = pl.ds(pl.multiple_of(gi * LANES, LANES), LANES)
        for r in range(rows_per_chunk):
            a_blk = a_sc[r * nkeys:(r + 1) * nkeys, lanes]
            act = 0.5 * a_blk * (1.0 + lax.erf(a_blk * math.sqrt(0.5)))
            gsum = jnp.zeros((nkeys, LANES), F32)
            rows = pl.ds(i1_base + (r // SUBLANES) * SUBLANES, SUBLANES)
            rr = r % SUBLANES
            for h in range(heads):
                cand = s1_sc[h, rows, lanes][rr:rr + 1, :] + s2_sc[h, :, lanes]
                gate = e1_sc[h, rows, lanes][rr:rr + 1, :] * e2_sc[h, :, lanes]
                gsum = gsum + jnp.where(cand >= thr_sc[h:h + 1, lanes], gate, 0.0)
            w_sc[r * nkeys:(r + 1) * nkeys, lanes] = (gsum * act).astype(BF16)
        return carry

    lax.fori_loop(0, groups, dense_group, 0)
    acc_sc[...] += _dot_tn(w_sc[...], v_ref[...])

    @pl.when(c == pl.num_programs(1) - 1)
    def _():
        o_ref[...] = x2_ref[...] + acc_sc[...]


def _peer(x2, h2, q, k1_bf, k2_bf, u_bf, v_bf, *, tn, ec):
    n, d = x2.shape
    heads, nkeys, half = k1_bf.shape
    n_exp = u_bf.shape[0]
    tn = min(tn, max(LANES, pl.cdiv(n, LANES) * LANES))
    assert ec % (SUBLANES * nkeys) == 0 and n_exp % ec == 0 and heads <= SUBLANES
    tok = lambda w: pl.BlockSpec((tn, w), lambda i, c: (i, 0))
    per_head = lambda: pltpu.VMEM((heads, nkeys, tn), F32)
    return pl.pallas_call(
        functools.partial(_peer_kernel, heads=heads, topk=PEER_TOPK, nkeys=nkeys),
        grid=(pl.cdiv(n, tn), n_exp // ec),
        in_specs=[tok(d), tok(d), tok(q.shape[1]),
                  pl.BlockSpec(k1_bf.shape, lambda i, c: (0, 0, 0)), pl.BlockSpec(k2_bf.shape, lambda i, c: (0, 0, 0)),
                  pl.BlockSpec((ec, d), lambda i, c: (c, 0)), pl.BlockSpec((ec, d), lambda i, c: (c, 0))],
        out_specs=tok(d),
        out_shape=jax.ShapeDtypeStruct((n, d), F32),
        scratch_shapes=[per_head(), per_head(), per_head(), per_head(), pltpu.VMEM((SUBLANES, tn), F32),
                        pltpu.VMEM((ec, tn), F32), pltpu.VMEM((ec, tn), BF16), pltpu.VMEM((tn, d), F32)],
        compiler_params=_cparams(("parallel", "arbitrary")),
        name="peer",
    )(x2, h2, q, k1_bf, k2_bf, u_bf, v_bf)


def _rmsnorm_kernel(x_ref, g_ref, o_ref):
    x = x_ref[...]
    o_ref[...] = x * lax.rsqrt(jnp.mean(x * x, axis=-1, keepdims=True) + EPS) * g_ref[...]


def _rmsnorm(x, g, *, tm):
    n, d = x.shape
    tm = min(tm, n)
    return pl.pallas_call(
        _rmsnorm_kernel,
        grid=(pl.cdiv(n, tm),),
        in_specs=[pl.BlockSpec((tm, d), lambda i: (i, 0)), _full((1, d))],
        out_specs=pl.BlockSpec((tm, d), lambda i: (i, 0)),
        out_shape=jax.ShapeDtypeStruct((n, d), F32),
        compiler_params=_cparams(("parallel",)),
        name="final_norm",
    )(x, g.reshape(1, d))


def _pad_rows(w, total, offset):
    return jnp.zeros((total, w.shape[1]), w.dtype).at[offset:offset + w.shape[0]].set(w)


def kernel(x_prompt, x_sample, cache_k, cache_v, state_conv, state_shift, state_wkv, page_table, meta_tokens, norm_mix, w_in, w_out, lam_q1, lam_k1, lam_q2, lam_k2, attn_subln, conv_dw, conv_dw_b, conv_ln_g, conv_ln_b, rw_mu, rw_w0, rw_w2, rw_a0, rw_a2, rw_g2, rw_v0, rw_v1, rw_v2, rw_kk, rw_ka, rw_rk, rw_lnx_g, rw_lnx_b, norm_ffn, peer_wq, peer_k1, peer_k2, peer_u, peer_v, norm_final):
    batch, seq, d = x_prompt.shape
    bs, t_new, _ = x_sample.shape
    depth = w_in.shape[0]
    n_meta = meta_tokens.shape[0]
    lp = n_meta + seq
    att_heads, att_vd = cache_v.shape[3], cache_v.shape[4]
    att_w = att_heads * att_vd
    hd_att = att_vd // 2
    conv_ch = conv_dw.shape[2]
    rw_heads, rw_hd = state_wkv.shape[2], state_wkv.shape[3]
    rw_w = rw_heads * rw_hd
    lw, la = rw_w2.shape[1], rw_a2.shape[1]
    lg = rw_g2.shape[1]
    lora_w = lw + la + lg
    n_pool, page = cache_k.shape[1], cache_k.shape[2]
    n_exp = peer_u.shape[1]

    tm = 512
    tq = min(256, seq)
    conv_tt = next(t for t in (912, 456, 152, 72, 24, 8) if lp % t == 0)
    conv_rc = next(r for r in (48, 24, 8) if conv_tt % r == 0)
    chunk_p = min(64, 1 << (lp.bit_length() - 1))
    chunk_s = max(2 * SUBLANES, 1 << (t_new - 1).bit_length())
    pages = min(8, page_table.shape[1])
    peer_ec = min(1024, n_exp)

    xp = jnp.concatenate([jnp.broadcast_to(meta_tokens[None], (batch, n_meta, d)), x_prompt], axis=1)
    xp = xp.reshape(batch * lp, d)
    xs = x_sample.transpose(1, 0, 2).reshape(t_new * bs, d)


    outs = {name: [] for name in ("kp", "vp", "cp", "sp", "Sp", "ks", "vs", "cs", "ss", "Ss")}
    pk0_p = pk0_s = None
    eye_h = jnp.eye(att_heads, dtype=BF16)
    eye_m = jnp.eye(2, dtype=BF16)
    ncol = max(LANES, att_heads * 2 * t_new)

    for l in range(depth):
        lam_i = _lambda_init(l)
        lam = (jnp.exp(jnp.sum(lam_q1[l] * lam_k1[l])) - jnp.exp(jnp.sum(lam_q2[l] * lam_k2[l])) + lam_i)
        lam = lam.astype(F32).reshape(1)
        w_in_bf = w_in[l].astype(BF16)
        wo_bf = w_out[l].astype(BF16)
        wq_bf = peer_wq[l].astype(BF16)
        k1_bf = peer_k1[l].astype(BF16)
        k2_bf = peer_k2[l].astype(BF16)
        u_bf = peer_u[l].astype(BF16)
        v_bf = peer_v[l].astype(BF16)
        rwp = {"rw_mu": rw_mu[l], "rw_w0": rw_w0[l], "rw_a0": rw_a0[l], "rw_kk": rw_kk[l], "rw_ka": rw_ka[l],
               "w2p": _pad_rows(rw_w2[l], lora_w, 0), "a2p": _pad_rows(rw_a2[l], lora_w, lw),
               "g2p": _pad_rows(rw_g2[l], lora_w, lw + la)}
        if l > 0:
            lv = rw_v1.shape[2]
            rwp["rw_v0"] = rw_v0[l - 1]
            rwp["v1p"] = jnp.zeros((rw_w, LANES), F32).at[:, :lv].set(rw_v1[l - 1])
            rwp["v2p"] = _pad_rows(rw_v2[l - 1], LANES, 0)

        proj = functools.partial(_norm_proj, g=norm_mix[l], w_bf=w_in_bf, att_w=att_w, conv_w2=2 * conv_ch,
                                 q_scale=hd_att ** -0.5, tm=tm)
        qkv_p, k_p, v_p, u_p, c_p = proj(xp)
        qkv_s, k_s, v_s, u_s, c_s = proj(xs)

        att_p = _attn_prompt(qkv_p, lam, attn_subln[l], batch=batch, heads=att_heads, n_meta=n_meta, tq=tq,
                             lam_init=lam_i)
        q_s = qkv_s[:, :att_w].reshape(t_new, bs, att_heads, 2, hd_att)
        qpp = jnp.einsum("tbhmd,hH,mM->bhmdHMt", q_s, eye_h, eye_m).reshape(bs, att_w, att_heads * 2 * t_new)
        qpp = jnp.pad(qpp, ((0, 0), (0, 0), (0, ncol - qpp.shape[2])))
        tm_to_seq = lambda a: jnp.pad(a.reshape(t_new, bs, -1).transpose(1, 0, 2),
                                      ((0, 0), (0, SUBLANES - t_new), (0, 0)))
        att_s = _attn_sample(page_table, lam, qpp, tm_to_seq(k_s), tm_to_seq(v_s), attn_subln[l], cache_k, cache_v, layer=l,
                             pages=pages, t_new=t_new, heads=att_heads, lam_init=lam_i)
        att_s = att_s.transpose(2, 0, 1, 3).reshape(t_new * bs, att_w)

        cy_p, nb_p = _conv_prompt(u_p, conv_dw[l], conv_dw_b[l], conv_ln_g[l], conv_ln_b[l], batch=batch,
                                  tt=conv_tt, rc=conv_rc)
        cy_s, nb_s = _conv_sample(u_s.reshape(t_new, bs, 2 * conv_ch), state_conv[l].transpose(1, 0, 2),
                                  conv_dw[l], conv_dw_b[l], conv_ln_g[l], conv_ln_b[l])
        cy_s = cy_s.reshape(t_new * bs, conv_ch)

        c_p3 = c_p.reshape(batch, lp, -1)
        prev_p = jnp.concatenate([jnp.zeros((batch, 1, c_p.shape[1]), F32), c_p3[:, :-1]], axis=1)
        prev_s = jnp.concatenate([state_shift[l], c_s[:-bs]], axis=0)
        pk_p = _rw_prep(c_p, prev_p.reshape(batch * lp, -1), pk0_p, rwp, rw_w=rw_w, hd=rw_hd, tm=tm)
        pk_s = _rw_prep(c_s, prev_s, pk0_s, rwp, rw_w=rw_w, hd=rw_hd, tm=tm)
        if l == 0:
            pk0_p, pk0_s = pk_p, pk_s
        ry_p, S_p = _wkv(pk_p.reshape(batch, lp, 7 * rw_w), jnp.zeros((batch, rw_heads, rw_hd, rw_hd), F32),
                         rw_rk[l], rw_lnx_g[l], rw_lnx_b[l], chunk=chunk_p, sb=batch)
        pk_s3 = jnp.pad(pk_s.reshape(t_new, bs, -1).transpose(1, 0, 2), ((0, 0), (0, chunk_s - t_new), (0, 0)))
        ry_s, S_s = _wkv(pk_s3, state_wkv[l], rw_rk[l], rw_lnx_g[l], rw_lnx_b[l], chunk=chunk_s,
                         sb=math.gcd(bs, SUBLANES))
        ry_s = ry_s[:, :t_new].transpose(1, 0, 2).reshape(t_new * bs, rw_w)

        x2_p, h2_p, qp_p = _out_proj(xp, att_p, cy_p, ry_p.reshape(batch * lp, rw_w), wo_bf, norm_ffn[l], wq_bf, tm=tm)
        x2_s, h2_s, qp_s = _out_proj(xs, att_s, cy_s, ry_s, wo_bf, norm_ffn[l], wq_bf, tm=tm)
        xp = _peer(x2_p, h2_p, qp_p, k1_bf, k2_bf, u_bf, v_bf, tn=512, ec=peer_ec)
        xs = _peer(x2_s, h2_s, qp_s, k1_bf, k2_bf, u_bf, v_bf, tn=512, ec=peer_ec)

        seq_major = lambda a: a.reshape(t_new, bs, -1).transpose(1, 0, 2)
        outs["kp"].append(k_p.reshape(batch, lp, att_heads, att_vd))
        outs["vp"].append(v_p.reshape(batch, lp, att_heads, att_vd))
        outs["cp"].append(nb_p)
        outs["sp"].append(c_p3[:, -1])
        outs["Sp"].append(S_p)
        outs["ks"].append(seq_major(k_s).reshape(bs, t_new, att_heads, att_vd))
        outs["vs"].append(seq_major(v_s).reshape(bs, t_new, att_heads, att_vd))
        outs["cs"].append(nb_s.transpose(1, 0, 2))
        outs["ss"].append(c_s[-bs:])
        outs["Ss"].append(S_s)

    y_prompt = _rmsnorm(xp, norm_final, tm=tm).reshape(batch, lp, d)[:, n_meta:]
    y_sample = _rmsnorm(xs, norm_final, tm=tm).reshape(t_new, bs, d).transpose(1, 0, 2)
    st = lambda name: jnp.stack(outs[name])
    return (y_prompt, y_sample, st("kp"), st("vp"), st("cp"), st("sp"), st("Sp"),
            st("ks"), st("vs"), st("cs"), st("ss"), st("Ss"))
```
